```python
import jax, jax.numpy as jnp
from jax import lax
import numpy as np

D_MODEL = 2048
BATCH = 16
SEQ = 2048
DEPTH = 1
DEC_BATCH = 8
DEC_SEQ = 32
PAST_LEN = 4096

CHUNK = 64
N_PREV_CHUNKS = 8
WINDOW = (N_PREV_CHUNKS + 1) * CHUNK
MIX_WIDTH = D_MODEL
ATTN_WIDTH = MIX_WIDTH // 2
CONV_CH = MIX_WIDTH - ATTN_WIDTH
HEAD_DIM = 64
N_HEADS = ATTN_WIDTH // HEAD_DIM
IN_WIDTH = 3 * ATTN_WIDTH + 2 * CONV_CH
CONV_K = 31
REL_CLIP = 128
N_GROUPS = 4
EXPERTS_PER_GROUP = 8
N_EXPERTS = N_GROUPS * EXPERTS_PER_GROUP
TOP_K = 2
D_EXPERT = D_MODEL // 2
PLE_DIM = 256
MOE_BLOCK = 128
EPS = 1e-6
NEG_INF = -1e30

kernel_name = "hymba_conformer_band_attn_hmoe_stream"


def rms_norm(x, g):
    xf = x.astype(jnp.float32)
    y = xf * lax.rsqrt(jnp.mean(xf * xf, axis=-1, keepdims=True) + EPS)
    return (y * g.astype(jnp.float32)).astype(x.dtype)


def layer_norm(x, g, b):
    xf = x.astype(jnp.float32)
    mu = jnp.mean(xf, axis=-1, keepdims=True)
    xc = xf - mu
    y = xc * lax.rsqrt(jnp.mean(xc * xc, axis=-1, keepdims=True) + EPS)
    return (y * g.astype(jnp.float32) + b.astype(jnp.float32)).astype(x.dtype)


def rel_bias_table(rel_bias, q_pos, k_pos):
    idx = jnp.clip(q_pos[:, None] - k_pos[None, :], -REL_CLIP, REL_CLIP) + REL_CLIP
    return rel_bias[:, idx]


def attend(q, k, v, bias, mask):
    s = jnp.einsum('bqhd,bkhd->bhqk', q, k).astype(jnp.float32) * (HEAD_DIM ** -0.5)
    s = s + bias[None].astype(jnp.float32)
    s = jnp.where(mask[None, None], s, NEG_INF)
    p = jax.nn.softmax(s, axis=-1).astype(v.dtype)
    return jnp.einsum('bhqk,bkhd->bqhd', p, v)


def in_proj(x, lw):
    B, S, _ = x.shape
    h = rms_norm(x, lw['g_mix'])
    u = h @ lw['w_in']
    q, k, v, a, b = jnp.split(u, [ATTN_WIDTH, 2 * ATTN_WIDTH, 3 * ATTN_WIDTH,
                                  3 * ATTN_WIDTH + CONV_CH], axis=-1)
    q = rms_norm(q.reshape(B, S, N_HEADS, HEAD_DIM), lw['qn_g'])
    k = rms_norm(k.reshape(B, S, N_HEADS, HEAD_DIM), lw['kn_g'])
    v = v.reshape(B, S, N_HEADS, HEAD_DIM)
    glu = a * jax.nn.sigmoid(b)
    return q, k, v, glu


def depthwise_conv(xpad, w, b):
    out = lax.conv_general_dilated(xpad, w[:, None, :].astype(xpad.dtype),
                                   window_strides=(1,), padding='VALID',
                                   dimension_numbers=('NWC', 'WIO', 'NWC'),
                                   feature_group_count=xpad.shape[-1])
    return out + b


def band_attention_prompt(q, k, v, rel_bias):
    B, S, H, Dh = q.shape
    nc = S // CHUNK
    past = N_PREV_CHUNKS * CHUNK
    pad = ((0, 0), (past, 0), (0, 0), (0, 0))
    kp = jnp.pad(k, pad)
    vp = jnp.pad(v, pad)
    off_q = jnp.arange(CHUNK)
    off_k = jnp.arange(WINDOW) - past
    bias = rel_bias_table(rel_bias, off_q, off_k)
    qc = q.reshape(B, nc, CHUNK, H, Dh).transpose(1, 0, 2, 3, 4)

    def one_chunk(args):
        c, qb = args
        start = c * CHUNK
        kb = lax.dynamic_slice_in_dim(kp, start, WINDOW, axis=1)
        vb = lax.dynamic_slice_in_dim(vp, start, WINDOW, axis=1)
        mask = (start + off_k >= 0)[None, :]
        return attend(qb, kb, vb, bias, mask)

    out = lax.map(one_chunk, (jnp.arange(nc), qc))
    return out.transpose(1, 0, 2, 3, 4).reshape(B, S, H, Dh)


def band_attention_sample(q, k_all, v_all, rel_bias):
    S = q.shape[1]
    L_all = k_all.shape[1]
    q_pos = PAST_LEN + jnp.arange(S)
    k_pos = PAST_LEN + S - L_all + jnp.arange(L_all)
    bias = rel_bias_table(rel_bias, q_pos, k_pos)
    dc = q_pos[:, None] // CHUNK - k_pos[None, :] // CHUNK
    mask = (dc >= 0) & (dc <= N_PREV_CHUNKS)
    return attend(q, k_all, v_all, bias, mask)


def hier_moe(h, lw):
    T, D = h.shape
    lg = (h @ lw['w_grp']).astype(jnp.float32) + lw['b_grp'].astype(jnp.float32)
    pg = jax.nn.softmax(lg, axis=-1)
    g_sel = jnp.argmax(lg, axis=-1)
    p_sel = jnp.take_along_axis(pg, g_sel[:, None], axis=-1)
    le = ((h @ lw['w_rt']).astype(jnp.float32) + lw['b_rt'].astype(jnp.float32))
    le = le.reshape(T, N_GROUPS, EXPERTS_PER_GROUP)
    within = jnp.take_along_axis(le, g_sel[:, None, None], axis=1)[:, 0]
    top_v, top_i = lax.top_k(within, TOP_K)
    gates = jax.nn.softmax(top_v, axis=-1) * p_sel
    eids = g_sel[:, None] * EXPERTS_PER_GROUP + top_i

    A = T * TOP_K
    e_flat = eids.reshape(-1).astype(jnp.int32)
    tok_flat = jnp.repeat(jnp.arange(T, dtype=jnp.int32), TOP_K)
    w_flat = gates.reshape(-1)
    order = jnp.argsort(e_flat)
    e_s, tok_s, w_s = e_flat[order], tok_flat[order], w_flat[order]
    counts = jnp.zeros((N_EXPERTS,), jnp.int32).at[e_flat].add(1)
    ends = jnp.cumsum(counts)
    starts = ends - counts
    pcounts = (counts + MOE_BLOCK - 1) // MOE_BLOCK * MOE_BLOCK
    pends = jnp.cumsum(pcounts)
    pstarts = pends - pcounts
    dest = pstarts[e_s] + (jnp.arange(A, dtype=jnp.int32) - starts[e_s])
    nb = -(-A // MOE_BLOCK) + N_EXPERTS
    buf_tok = jnp.full((nb * MOE_BLOCK,), T, jnp.int32).at[dest].set(tok_s)
    buf_w = jnp.zeros((nb * MOE_BLOCK,), h.dtype).at[dest].set(w_s.astype(h.dtype))
    block_e = jnp.minimum(jnp.searchsorted(pends, jnp.arange(nb, dtype=jnp.int32) * MOE_BLOCK,
                                           side='right'), N_EXPERTS - 1)
    h_pad = jnp.concatenate([h, jnp.zeros((1, D), h.dtype)], axis=0)

    def run_block(args):
        be, bt = args
        xb = h_pad[bt]
        g = xb @ lw['w_e_gate'][be]
        u = xb @ lw['w_e_up'][be]
        return (jax.nn.silu(g) * u) @ lw['w_e_down'][be]

    out = lax.map(run_block, (block_e, buf_tok.reshape(nb, MOE_BLOCK)))
    y = jnp.zeros((T + 1, D), h.dtype).at[buf_tok].add(out.reshape(-1, D) * buf_w[:, None])
    return y[:T]


def finish_layer(x, attn, conv_raw, p, lw):
    B, S, D = x.shape
    conv = jax.nn.silu(layer_norm(conv_raw, lw['cln_g'], lw['cln_b']))
    mix = jnp.concatenate([attn.reshape(B, S, ATTN_WIDTH), conv], axis=-1) @ lw['w_out']
    x = x + mix
    h = rms_norm(x, lw['g_ffn'])
    x = x + hier_moe(h.reshape(B * S, D), lw).reshape(B, S, D)
    gate = jax.nn.sigmoid(rms_norm(x, lw['g_ple']) @ lw['w_pg'] + lw['b_pg'])
    return x + gate * (p @ lw['w_pe'])


def setup_inputs(seed: int = 0) -> dict:
    key = jax.random.key(seed)
    ks = jax.random.split(key, 32)
    f32 = jnp.float32
    kv_buf = min(WINDOW, PAST_LEN)
    nrm = lambda k, shape, s: jax.random.normal(k, shape, f32) * s
    gain = lambda k, shape: 1.0 + 0.05 * jax.random.normal(k, shape, f32)
    return {
        "x_prompt": nrm(ks[0], (BATCH, SEQ, D_MODEL), 1.0),
        "x_sample": nrm(ks[1], (DEC_BATCH, DEC_SEQ, D_MODEL), 1.0),
        "p_prompt": nrm(ks[2], (DEPTH, BATCH, SEQ, PLE_DIM), 1.0),
        "p_sample": nrm(ks[3], (DEPTH, DEC_BATCH, DEC_SEQ, PLE_DIM), 1.0),
        "cache_k": nrm(ks[4], (DEPTH, DEC_BATCH, kv_buf, N_HEADS, HEAD_DIM), 1.0),
        "cache_v": nrm(ks[5], (DEPTH, DEC_BATCH, kv_buf, N_HEADS, HEAD_DIM), 1.0),
        "cache_conv": nrm(ks[6], (DEPTH, DEC_BATCH, CONV_K - 1, CONV_CH), 0.5),
        "g_mix": gain(ks[7], (DEPTH, D_MODEL)),
        "w_in": nrm(ks[8], (DEPTH, D_MODEL, IN_WIDTH), D_MODEL ** -0.5),
        "qn_g": gain(ks[9], (DEPTH, HEAD_DIM)),
        "kn_g": gain(ks[10], (DEPTH, HEAD_DIM)),
        "rel_bias": nrm(ks[11], (DEPTH, N_HEADS, 2 * REL_CLIP + 1), 0.2),
        "conv_w": nrm(ks[12], (DEPTH, CONV_K, CONV_CH), CONV_K ** -0.5),
        "conv_b": nrm(ks[13], (DEPTH, CONV_CH), 0.02),
        "cln_g": gain(ks[14], (DEPTH, CONV_CH)),
        "cln_b": nrm(ks[15], (DEPTH, CONV_CH), 0.02),
        "w_out": nrm(ks[16], (DEPTH, MIX_WIDTH, D_MODEL), MIX_WIDTH ** -0.5),
        "g_ffn": gain(ks[17], (DEPTH, D_MODEL)),
        "w_grp": nrm(ks[18], (DEPTH, D_MODEL, N_GROUPS), D_MODEL ** -0.5),
        "b_grp": nrm(ks[19], (DEPTH, N_GROUPS), 0.01),
        "w_rt": nrm(ks[20], (DEPTH, D_MODEL, N_EXPERTS), D_MODEL ** -0.5),
        "b_rt": nrm(ks[21], (DEPTH, N_EXPERTS), 0.01),
        "w_e_gate": nrm(ks[22], (DEPTH, N_EXPERTS, D_MODEL, D_EXPERT), D_MODEL ** -0.5),
        "w_e_up": nrm(ks[23], (DEPTH, N_EXPERTS, D_MODEL, D_EXPERT), D_MODEL ** -0.5),
        "w_e_down": nrm(ks[24], (DEPTH, N_EXPERTS, D_EXPERT, D_MODEL), D_EXPERT ** -0.5),
        "g_ple": gain(ks[25], (DEPTH, D_MODEL)),
        "w_pg": nrm(ks[26], (DEPTH, D_MODEL, D_MODEL), D_MODEL ** -0.5),
        "b_pg": nrm(ks[27], (DEPTH, D_MODEL), 0.02),
        "w_pe": nrm(ks[28], (DEPTH, PLE_DIM, D_MODEL), PLE_DIM ** -0.5),
    }


def reference(x_prompt, x_sample, p_prompt, p_sample, cache_k, cache_v, cache_conv,
              g_mix, w_in, qn_g, kn_g, rel_bias, conv_w, conv_b, cln_g, cln_b, w_out,
              g_ffn, w_grp, b_grp, w_rt, b_rt, w_e_gate, w_e_up, w_e_down,
              g_ple, w_pg, b_pg, w_pe):
    y_p, y_s = x_prompt, x_sample
    nk_p, nv_p, nc_p, nk_s, nv_s, nc_s = [], [], [], [], [], []
    for i in range(DEPTH):
        lw = {'g_mix': g_mix[i], 'w_in': w_in[i], 'qn_g': qn_g[i], 'kn_g': kn_g[i],
              'cln_g': cln_g[i], 'cln_b': cln_b[i], 'w_out': w_out[i], 'g_ffn': g_ffn[i],
              'w_grp': w_grp[i], 'b_grp': b_grp[i], 'w_rt': w_rt[i], 'b_rt': b_rt[i],
              'w_e_gate': w_e_gate[i], 'w_e_up': w_e_up[i], 'w_e_down': w_e_down[i],
              'g_ple': g_ple[i], 'w_pg': w_pg[i], 'b_pg': b_pg[i], 'w_pe': w_pe[i]}
        q, k, v, glu = in_proj(y_p, lw)
        attn = band_attention_prompt(q, k, v, rel_bias[i])
        conv_in = jnp.pad(glu, ((0, 0), (CONV_K - 1, 0), (0, 0)))
        conv_raw = depthwise_conv(conv_in, conv_w[i], conv_b[i])
        keep_p = min(WINDOW, k.shape[1])
        nk_p.append(k[:, -keep_p:])
        nv_p.append(v[:, -keep_p:])
        nc_p.append(conv_in[:, -(CONV_K - 1):])
        y_p = finish_layer(y_p, attn, conv_raw, p_prompt[i], lw)
        q, k, v, glu = in_proj(y_s, lw)
        k_all = jnp.concatenate([cache_k[i].astype(k.dtype), k], axis=1)
        v_all = jnp.concatenate([cache_v[i].astype(v.dtype), v], axis=1)
        attn = band_attention_sample(q, k_all, v_all, rel_bias[i])
        conv_in = jnp.concatenate([cache_conv[i].astype(glu.dtype), glu], axis=1)
        conv_raw = depthwise_conv(conv_in, conv_w[i], conv_b[i])
        keep_s = min(WINDOW, k_all.shape[1])
        nk_s.append(k_all[:, -keep_s:])
        nv_s.append(v_all[:, -keep_s:])
        nc_s.append(conv_in[:, -(CONV_K - 1):])
        y_s = finish_layer(y_s, attn, conv_raw, p_sample[i], lw)
    new_k_prompt = jnp.stack(nk_p)
    new_v_prompt = jnp.stack(nv_p)
    new_conv_prompt = jnp.stack(nc_p)
    new_k_sample = jnp.stack(nk_s)
    new_v_sample = jnp.stack(nv_s)
    new_conv_sample = jnp.stack(nc_s)
    return (y_p, y_s, new_k_prompt, new_v_prompt, new_conv_prompt,
            new_k_sample, new_v_sample, new_conv_sample)
```

```python
import functools

import numpy as np
import jax
import jax.numpy as jnp
from jax import lax
from jax.experimental import pallas as pl
from jax.experimental.pallas import tpu as pltpu

F32 = jnp.float32
BF16 = jnp.bfloat16

D_MODEL = 2048
ATTN_WIDTH = 1024
CONV_CH = 1024
HEAD_DIM = 64
N_HEADS = 16
N_PAIRS = N_HEADS // 2
IN_WIDTH = 3 * ATTN_WIDTH + 2 * CONV_CH
CHUNK = 64
N_PREV_CHUNKS = 8
WINDOW = (N_PREV_CHUNKS + 1) * CHUNK
BAND = WINDOW + CHUNK
CONV_K = 31
HALO = 32
REL_CLIP = 128
PAST_LEN = 4096
N_GROUPS = 4
EXPERTS_PER_GROUP = 8
N_EXPERTS = 32
D_EXPERT = 1024
PLE_DIM = 256
EPS = 1e-6
NEG_INF = -1e30
LANES = 128
ROUTE_W = 128

VMEM_LIMIT = 56 * 1024 * 1024


def _cparams(sem):
    return pltpu.CompilerParams(dimension_semantics=sem, vmem_limit_bytes=VMEM_LIMIT)


def _rms(x, g):
    return x * lax.rsqrt(jnp.mean(x * x, axis=-1, keepdims=True) + EPS) * g


def _in_proj_body(x_ref, gmix_ref, w_ref, qg_ref, kg_ref, sgn_ref,
                  q_ref, k_ref, v_ref, glu_ref, kf_ref, vf_ref, gf_ref):
    h = _rms(x_ref[...], gmix_ref[...]).astype(BF16)
    sgn = sgn_ref[...]

    def seg(i):
        return jnp.dot(h, w_ref[:, i * ATTN_WIDTH:(i + 1) * ATTN_WIDTH], preferred_element_type=F32)

    def head_norm(u, g_ref, p):
        up = u[:, p * LANES:(p + 1) * LANES]
        sq = up * up
        s = jnp.sum(sq, axis=-1, keepdims=True)
        d = jnp.sum(sq * sgn, axis=-1, keepdims=True)
        ms = (s + sgn * d) * (0.5 / HEAD_DIM)
        return up * lax.rsqrt(ms + EPS) * g_ref[:, p * LANES:(p + 1) * LANES]

    uq = seg(0)
    for p in range(N_PAIRS):
        q_ref[:, p * LANES:(p + 1) * LANES] = (head_norm(uq, qg_ref, p) * (HEAD_DIM ** -0.5)).astype(BF16)
    uk = seg(1)
    for p in range(N_PAIRS):
        kn = head_norm(uk, kg_ref, p)
        kf_ref[:, p * LANES:(p + 1) * LANES] = kn
        k_ref[:, p * LANES:(p + 1) * LANES] = kn.astype(BF16)
    uv = seg(2)
    vf_ref[...] = uv
    v_ref[...] = uv.astype(BF16)
    glu = seg(3) * jax.nn.sigmoid(seg(4))
    glu_ref[...] = glu.astype(BF16)
    gr = gf_ref.shape[0]
    gf_ref[...] = glu[glu.shape[0] - gr:, :]


def _in_proj(x2d, gmix, w_in, qg, kg, sgn, *, tm, tail_rows, kf_map, gf_rows, gf_blocks, gf_map):
    T = x2d.shape[0]
    nt = T // tm
    row = lambda i: (i, 0)
    const = lambda i: (0, 0)
    out_shape = (
        jax.ShapeDtypeStruct((T, ATTN_WIDTH), BF16),
        jax.ShapeDtypeStruct((T, ATTN_WIDTH), BF16),
        jax.ShapeDtypeStruct((T, ATTN_WIDTH), BF16),
        jax.ShapeDtypeStruct((T, CONV_CH), BF16),
        jax.ShapeDtypeStruct((tail_rows, ATTN_WIDTH), F32),
        jax.ShapeDtypeStruct((tail_rows, ATTN_WIDTH), F32),
        jax.ShapeDtypeStruct((gf_blocks * gf_rows, CONV_CH), F32),
    )
    return pl.pallas_call(
        _in_proj_body,
        grid=(nt,),
        in_specs=[
            pl.BlockSpec((tm, D_MODEL), row),
            pl.BlockSpec((1, D_MODEL), const),
            pl.BlockSpec((D_MODEL, IN_WIDTH), const, pipeline_mode=pl.Buffered(1)),
            pl.BlockSpec((1, ATTN_WIDTH), const),
            pl.BlockSpec((1, ATTN_WIDTH), const),
            pl.BlockSpec((1, LANES), const),
        ],
        out_specs=(
            pl.BlockSpec((tm, ATTN_WIDTH), row),
            pl.BlockSpec((tm, ATTN_WIDTH), row),
            pl.BlockSpec((tm, ATTN_WIDTH), row),
            pl.BlockSpec((tm, CONV_CH), row),
            pl.BlockSpec((tm, ATTN_WIDTH), kf_map),
            pl.BlockSpec((tm, ATTN_WIDTH), kf_map),
            pl.BlockSpec((gf_rows, CONV_CH), gf_map),
        ),
        out_shape=out_shape,
        compiler_params=_cparams(("arbitrary",)),
        name="in_proj",
    )(x2d, gmix, w_in, qg, kg, sgn)


CONV_ROWS = 64


def _conv_body(g_ref, gh_ref, h0_ref, w_ref, b_ref, lg_ref, lb_ref, o_ref, xc_ref, y_ref):
    j = pl.program_id(1)
    ts = g_ref.shape[1]
    first = j == 0
    xc_ref[0:HALO, :] = jnp.where(first, h0_ref[0], gh_ref[0].astype(F32))
    xc_ref[HALO:, :] = g_ref[0].astype(F32)
    off = HALO - (CONV_K - 1)
    rc = min(CONV_ROWS, ts)
    for p in range(CONV_CH // LANES):
        lanes = slice(p * LANES, (p + 1) * LANES)
        wp = w_ref[:, lanes]
        bp = b_ref[:, lanes]
        for r in range(ts // rc):
            r0 = r * rc
            acc = jnp.broadcast_to(bp, (rc, LANES))
            for t in range(CONV_K):
                acc = acc + wp[t:t + 1, :] * xc_ref[r0 + off + t:r0 + off + t + rc, lanes]
            y_ref[r0:r0 + rc, lanes] = acc
    y = y_ref[...]
    mu = jnp.mean(y, axis=-1, keepdims=True)
    yc = y - mu
    z = yc * lax.rsqrt(jnp.mean(yc * yc, axis=-1, keepdims=True) + EPS) * lg_ref[...] + lb_ref[...]
    o_ref[0] = (z * jax.nn.sigmoid(z)).astype(BF16)


def _conv_module(glu3, halo0, conv_w32, conv_b, cln_g, cln_b, *, ts):
    B, S, C = glu3.shape
    nj = S // ts
    hb = ts // HALO
    const = lambda b, j: (0, 0)
    return pl.pallas_call(
        _conv_body,
        grid=(B, nj),
        in_specs=[
            pl.BlockSpec((1, ts, C), lambda b, j: (b, j, 0)),
            pl.BlockSpec((1, HALO, C), lambda b, j: (b, jnp.maximum(j * hb - 1, 0), 0)),
            pl.BlockSpec((1, HALO, C), lambda b, j: (b, 0, 0)),
            pl.BlockSpec((HALO, C), const),
            pl.BlockSpec((1, C), const),
            pl.BlockSpec((1, C), const),
            pl.BlockSpec((1, C), const),
        ],
        out_specs=pl.BlockSpec((1, ts, C), lambda b, j: (b, j, 0)),
        out_shape=jax.ShapeDtypeStruct((B, S, C), BF16),
        scratch_shapes=[pltpu.VMEM((ts + HALO, C), F32), pltpu.VMEM((ts, C), F32)],
        compiler_params=_cparams(("arbitrary", "arbitrary")),
        name="conv_module",
    )(glu3, glu3, halo0, conv_w32, conv_b, cln_g, cln_b)


def _pair_attend(qp, kb, vb, bias, first_valid):
    R = qp.shape[0]
    lane = lax.broadcasted_iota(jnp.int32, (R, LANES), 1)
    zero = jnp.zeros_like(qp)
    qq = jnp.concatenate([jnp.where(lane < HEAD_DIM, qp, zero),
                          jnp.where(lane >= HEAD_DIM, qp, zero)], axis=0)
    s = lax.dot_general(qq, kb, (((1,), (1,)), ((), ())), preferred_element_type=F32)
    s = s + bias
    if first_valid is not None:
        kidx = lax.broadcasted_iota(jnp.int32, s.shape, 1)
        s = jnp.where(kidx >= first_valid, s, NEG_INF)
    m = jnp.max(s, axis=-1, keepdims=True)
    e = jnp.exp(s - m)
    l = jnp.sum(e, axis=-1, keepdims=True)
    o2 = jnp.dot(e.astype(BF16), vb, preferred_element_type=F32) * (1.0 / l)
    return jnp.where(lane < HEAD_DIM, o2[:R], o2[R:])


def _attn_prompt_body(q_ref, kp_ref, kc_ref, vp_ref, vc_ref, bias_ref, o_ref, kcat, vcat):
    j = pl.program_id(1)
    tq = q_ref.shape[0]
    nch = tq // CHUNK
    kcat[0:CHUNK, :] = jnp.zeros((CHUNK, ATTN_WIDTH), BF16)
    vcat[0:CHUNK, :] = jnp.zeros((CHUNK, ATTN_WIDTH), BF16)
    kcat[CHUNK:CHUNK + tq, :] = kp_ref[...]
    vcat[CHUNK:CHUNK + tq, :] = vp_ref[...]
    kcat[CHUNK + tq:, :] = kc_ref[...]
    vcat[CHUNK + tq:, :] = vc_ref[...]
    hist = jnp.where(j == 0, 0, tq)

    def chunk(c, carry):
        r0 = pl.multiple_of(c * CHUNK, CHUNK)
        first_valid = CHUNK + tq - hist - c * CHUNK
        for p in range(N_PAIRS):
            lanes = slice(p * LANES, (p + 1) * LANES)
            o = _pair_attend(q_ref[pl.ds(r0, CHUNK), lanes],
                             kcat[pl.ds(r0, BAND), lanes],
                             vcat[pl.ds(r0, BAND), lanes],
                             bias_ref[p], first_valid)
            o_ref[pl.ds(r0, CHUNK), lanes] = o.astype(BF16)
        return carry

    lax.fori_loop(0, nch, chunk, 0)


def _attn_prompt(q, k, v, bias, *, B, S, tq):
    assert tq == WINDOW - CHUNK, "key band of a query tile = previous tile + own tile"
    nj = S // tq
    cur = lambda b, j: (b * nj + j, 0)
    prev = lambda b, j: (b * nj + jnp.maximum(j - 1, 0), 0)
    blk = (tq, ATTN_WIDTH)
    return pl.pallas_call(
        _attn_prompt_body,
        grid=(B, nj),
        in_specs=[
            pl.BlockSpec(blk, cur),
            pl.BlockSpec(blk, prev),
            pl.BlockSpec(blk, cur),
            pl.BlockSpec(blk, prev),
            pl.BlockSpec(blk, cur),
            pl.BlockSpec((N_PAIRS, 2 * CHUNK, BAND), lambda b, j: (0, 0, 0)),
        ],
        out_specs=pl.BlockSpec(blk, cur),
        out_shape=jax.ShapeDtypeStruct((B * S, ATTN_WIDTH), BF16),
        scratch_shapes=[pltpu.VMEM((CHUNK + 2 * tq, ATTN_WIDTH), BF16),
                        pltpu.VMEM((CHUNK + 2 * tq, ATTN_WIDTH), BF16)],
        compiler_params=_cparams(("arbitrary", "arbitrary")),
        name="attn_prompt",
    )(q, k, k, v, v, bias)


def _attn_sample_body(q_ref, k_ref, v_ref, bias_ref, o_ref):
    for p in range(N_PAIRS):
        lanes = slice(p * LANES, (p + 1) * LANES)
        o = _pair_attend(q_ref[0, :, lanes], k_ref[0, :, lanes], v_ref[0, :, lanes], bias_ref[p], None)
        o_ref[0, :, lanes] = o.astype(BF16)


def _attn_sample(q3, k3, v3, bias):
    B, R, _ = q3.shape
    return pl.pallas_call(
        _attn_sample_body,
        grid=(B,),
        in_specs=[
            pl.BlockSpec((1, R, ATTN_WIDTH), lambda b: (b, 0, 0)),
            pl.BlockSpec((1, BAND, ATTN_WIDTH), lambda b: (b, 0, 0)),
            pl.BlockSpec((1, BAND, ATTN_WIDTH), lambda b: (b, 0, 0)),
            pl.BlockSpec((N_PAIRS, 2 * R, BAND), lambda b: (0, 0, 0)),
        ],
        out_specs=pl.BlockSpec((1, R, ATTN_WIDTH), lambda b: (b, 0, 0)),
        out_shape=jax.ShapeDtypeStruct((B, R, ATTN_WIDTH), BF16),
        compiler_params=_cparams(("arbitrary",)),
        name="attn_sample",
    )(q3, k3, v3, bias)


def _split_bf16(x):
    hi = x.astype(BF16)
    lo = (x - hi.astype(F32)).astype(BF16)
    return hi, lo


def _out_proj_body(ap_ref, as_ref, cp_ref, cs_ref, xp_ref, xs_ref, wa_ref, wc_ref, gffn_ref,
                   wrh_ref, wrl_ref, br_ref, xm_ref, route_ref, *, ntp):
    is_p = pl.program_id(0) < ntp
    a = jnp.where(is_p, ap_ref[...], as_ref[...])
    c = jnp.where(is_p, cp_ref[...], cs_ref[...])
    x = jnp.where(is_p, xp_ref[...], xs_ref[...])
    mix = jnp.dot(a, wa_ref[...], preferred_element_type=F32)
    mix = mix + jnp.dot(c, wc_ref[...], preferred_element_type=F32)
    xm = x + mix
    xm_ref[...] = xm
    h = _rms(xm, gffn_ref[...])
    hh, hl = _split_bf16(h)
    wh = wrh_ref[...]
    lg = (jnp.dot(hh, wh, preferred_element_type=F32)
          + jnp.dot(hl, wh, preferred_element_type=F32)
          + jnp.dot(hh, wrl_ref[...], preferred_element_type=F32)) + br_ref[...]
    lane = lax.broadcasted_iota(jnp.int32, lg.shape, 1)
    big = jnp.int32(1 << 20)

    def top1(mask):
        v = jnp.max(jnp.where(mask, lg, -jnp.inf), axis=-1, keepdims=True)
        i = jnp.min(jnp.where(mask & (lg == v), lane, big), axis=-1, keepdims=True)
        return v, i

    gmask = lane < N_GROUPS
    gv, gi = top1(gmask)
    p_sel = 1.0 / jnp.sum(jnp.where(gmask, jnp.exp(lg - gv), 0.0), axis=-1, keepdims=True)
    lo_lane = N_GROUPS + gi * EXPERTS_PER_GROUP
    emask = (lane >= lo_lane) & (lane < lo_lane + EXPERTS_PER_GROUP)
    v1, i1 = top1(emask)
    v2, i2 = top1(emask & (lane != i1))
    e2 = jnp.exp(v2 - v1)
    g1 = p_sel / (1.0 + e2)
    g2 = p_sel * e2 / (1.0 + e2)
    route = jnp.where(lane == 0, (i1 - N_GROUPS).astype(F32),
                      jnp.where(lane == 1, (i2 - N_GROUPS).astype(F32),
                                jnp.where(lane == 2, g1, jnp.where(lane == 3, g2, 0.0))))
    route_ref[...] = route


def _out_proj(attn_p, attn_s, conv_p, conv_s, x_p, x_s, w_oa, w_oc, gffn, wr_hi, wr_lo, b_r, *, tm):
    Tp, Ts = x_p.shape[0], x_s.shape[0]
    ntp, nts = Tp // tm, Ts // tm
    pmap = lambda i: (jnp.minimum(i, ntp - 1), 0)
    smap = lambda i: (jnp.maximum(i - ntp, 0), 0)
    row = lambda i: (i, 0)
    const = lambda i: (0, 0)
    return pl.pallas_call(
        functools.partial(_out_proj_body, ntp=ntp),
        grid=(ntp + nts,),
        in_specs=[
            pl.BlockSpec((tm, ATTN_WIDTH), pmap),
            pl.BlockSpec((tm, ATTN_WIDTH), smap),
            pl.BlockSpec((tm, CONV_CH), pmap),
            pl.BlockSpec((tm, CONV_CH), smap),
            pl.BlockSpec((tm, D_MODEL), pmap),
            pl.BlockSpec((tm, D_MODEL), smap),
            pl.BlockSpec((ATTN_WIDTH, D_MODEL), const),
            pl.BlockSpec((CONV_CH, D_MODEL), const),
            pl.BlockSpec((1, D_MODEL), const),
            pl.BlockSpec((D_MODEL, ROUTE_W), const),
            pl.BlockSpec((D_MODEL, ROUTE_W), const),
            pl.BlockSpec((1, ROUTE_W), const),
        ],
        out_specs=(pl.BlockSpec((tm, D_MODEL), row), pl.BlockSpec((tm, ROUTE_W), row)),
        out_shape=(jax.ShapeDtypeStruct((Tp + Ts, D_MODEL), F32),
                   jax.ShapeDtypeStruct((Tp + Ts, ROUTE_W), F32)),
        compiler_params=_cparams(("arbitrary",)),
        name="out_proj",
    )(attn_p, attn_s, conv_p, conv_s, x_p, x_s, w_oa, w_oc, gffn, wr_hi, wr_lo, b_r)


def _moe_body(be_ref, nu_ref, tok_ref, xm_hbm, gffn_ref, wg_ref, wu_ref, wd_ref, out_ref, xbuf, sem):
    b = pl.program_id(0)
    tm = out_ref.shape[0]
    nused = nu_ref[0]

    def issue(blk, slot):
        def body(r, carry):
            tok = tok_ref[blk * tm + r]
            pltpu.make_async_copy(xm_hbm.at[pl.ds(tok, 1)], xbuf.at[slot, pl.ds(r, 1)], sem.at[slot]).start()
            return carry
        lax.fori_loop(0, tm, body, 0)

    @pl.when(b == 0)
    def _():
        issue(0, 0)

    @pl.when(b < nused)
    def _():
        slot = b % 2

        @pl.when(b + 1 < nused)
        def _():
            issue(b + 1, 1 - slot)

        pltpu.make_async_copy(xm_hbm.at[pl.ds(0, tm)], xbuf.at[slot], sem.at[slot]).wait()
        h = _rms(xbuf[slot], gffn_ref[...]).astype(BF16)
        g = jnp.dot(h, wg_ref[0], preferred_element_type=F32)
        u = jnp.dot(h, wu_ref[0], preferred_element_type=F32)
        a = (g * jax.nn.sigmoid(g) * u).astype(BF16)
        out_ref[...] = jnp.dot(a, wd_ref[0], preferred_element_type=F32)

    @pl.when(b >= nused)
    def _():
        out_ref[...] = jnp.zeros(out_ref.shape, F32)


def _moe(block_e, nused, slot_tok, xm_all, gffn, wg, wu, wd, *, tm, nb):
    wmap = lambda b, be, nu, tok: (be[b], 0, 0)
    grid_spec = pltpu.PrefetchScalarGridSpec(
        num_scalar_prefetch=3,
        grid=(nb,),
        in_specs=[
            pl.BlockSpec(memory_space=pl.ANY),
            pl.BlockSpec((1, D_MODEL), lambda b, be, nu, tok: (0, 0)),
            pl.BlockSpec((1, D_MODEL, D_EXPERT), wmap),
            pl.BlockSpec((1, D_MODEL, D_EXPERT), wmap),
            pl.BlockSpec((1, D_EXPERT, D_MODEL), wmap),
        ],
        out_specs=pl.BlockSpec((tm, D_MODEL), lambda b, be, nu, tok: (b, 0)),
        scratch_shapes=[pltpu.VMEM((2, tm, D_MODEL), F32), pltpu.SemaphoreType.DMA((2,))],
    )
    return pl.pallas_call(
        _moe_body,
        grid_spec=grid_spec,
        out_shape=jax.ShapeDtypeStruct((nb * tm, D_MODEL), F32),
        compiler_params=_cparams(("arbitrary",)),
        name="moe",
    )(block_e, nused, slot_tok, xm_all, gffn, wg, wu, wd)


def _final_body(dest_ref, xm_ref, route_ref, p_ref, mo_hbm, gple_ref, wpg_ref, bpg_ref, wpe_ref,
                y_ref, ybuf, sem, *, tok0):
    i = pl.program_id(0)
    nt = pl.num_programs(0)
    tm = y_ref.shape[0]

    def issue(tile, slot):
        def body(r, carry):
            base = 2 * (tok0 + tile * tm + r)
            for kk in range(2):
                d = dest_ref[base + kk]
                pltpu.make_async_copy(mo_hbm.at[pl.ds(d, 1)], ybuf.at[slot, kk, pl.ds(r, 1)], sem.at[slot]).start()
            return carry
        lax.fori_loop(0, tm, body, 0)

    @pl.when(i == 0)
    def _():
        issue(0, 0)

    slot = i % 2

    @pl.when(i + 1 < nt)
    def _():
        issue(i + 1, 1 - slot)

    for kk in range(2):
        pltpu.make_async_copy(mo_hbm.at[pl.ds(0, tm)], ybuf.at[slot, kk], sem.at[slot]).wait()
    route = route_ref[...]
    x2 = xm_ref[...] + route[:, 2:3] * ybuf[slot, 0] + route[:, 3:4] * ybuf[slot, 1]
    r = _rms(x2, gple_ref[...]).astype(BF16)
    z = jnp.dot(r, wpg_ref[...], preferred_element_type=F32) + bpg_ref[...]
    pe = jnp.dot(p_ref[...].astype(BF16), wpe_ref[...], preferred_element_type=F32)
    y_ref[...] = x2 + jax.nn.sigmoid(z) * pe


def _final(dest, xm_all, route, p2d, moe_out, gple, wpg, bpg, wpe, *, tm, blk0):
    T = p2d.shape[0]
    nt = T // tm
    const = lambda i, d: (0, 0)
    grid_spec = pltpu.PrefetchScalarGridSpec(
        num_scalar_prefetch=1,
        grid=(nt,),
        in_specs=[
            pl.BlockSpec((tm, D_MODEL), lambda i, d: (i + blk0, 0)),
            pl.BlockSpec((tm, ROUTE_W), lambda i, d: (i + blk0, 0)),
            pl.BlockSpec((tm, PLE_DIM), lambda i, d: (i, 0)),
            pl.BlockSpec(memory_space=pl.ANY),
            pl.BlockSpec((1, D_MODEL), const),
            pl.BlockSpec((D_MODEL, D_MODEL), const),
            pl.BlockSpec((1, D_MODEL), const),
            pl.BlockSpec((PLE_DIM, D_MODEL), const),
        ],
        out_specs=pl.BlockSpec((tm, D_MODEL), lambda i, d: (i, 0)),
        scratch_shapes=[pltpu.VMEM((2, 2, tm, D_MODEL), F32), pltpu.SemaphoreType.DMA((2,))],
    )
    return pl.pallas_call(
        functools.partial(_final_body, tok0=blk0 * tm),
        grid_spec=grid_spec,
        out_shape=jax.ShapeDtypeStruct((T, D_MODEL), F32),
        compiler_params=_cparams(("arbitrary",)),
        name="final",
    )(dest, xm_all, route, p2d, moe_out, gple, wpg, bpg, wpe)


def _pair_stack(bias):
    H, R, K = bias.shape
    return bias.reshape(H // 2, 2 * R, K)


def _prompt_bias(rel_bias):
    off_q = np.arange(CHUNK)
    off_k = np.arange(BAND) - (BAND - CHUNK)
    idx = np.clip(off_q[:, None] - off_k[None, :], -REL_CLIP, REL_CLIP) + REL_CLIP
    tbl = rel_bias[:, idx]
    tbl = jnp.where((np.arange(BAND) >= CHUNK)[None, None, :], tbl, NEG_INF)
    return _pair_stack(tbl.astype(F32))


def _sample_bias(rel_bias, S, L_all):
    q_pos = PAST_LEN + np.arange(S)
    k_pos = PAST_LEN + S - L_all + np.arange(BAND)
    idx = np.clip(q_pos[:, None] - k_pos[None, :], -REL_CLIP, REL_CLIP) + REL_CLIP
    dc = q_pos[:, None] // CHUNK - k_pos[None, :] // CHUNK
    mask = (dc >= 0) & (dc <= N_PREV_CHUNKS) & (np.arange(BAND) < L_all)[None, :]
    tbl = jnp.where(mask[None], rel_bias[:, idx], NEG_INF)
    return _pair_stack(tbl.astype(F32))


def _routing_tables(route_all, *, tm, nb):
    T_all = route_all.shape[0]
    A = 2 * T_all
    e_flat = route_all[:, 0:2].astype(jnp.int32).reshape(A)
    onehot = (e_flat[:, None] == jnp.arange(N_EXPERTS, dtype=jnp.int32)[None, :]).astype(jnp.int32)
    csum = jnp.cumsum(onehot, axis=0)
    rank = jnp.sum(csum * onehot, axis=1) - 1
    counts = csum[-1]
    pcounts = (counts + tm - 1) // tm * tm
    pends = jnp.cumsum(pcounts)
    pstarts = pends - pcounts
    dest = (pstarts[e_flat] + rank).astype(jnp.int32)
    nused = (pends[-1] // tm).astype(jnp.int32)
    blk = jnp.minimum(jnp.arange(nb, dtype=jnp.int32), nused - 1)
    block_e = jnp.minimum(jnp.searchsorted(pends, blk * tm, side='right'), N_EXPERTS - 1).astype(jnp.int32)
    slot_tok = jnp.zeros((nb * tm,), jnp.int32).at[dest].set(
        jnp.arange(A, dtype=jnp.int32) // 2, unique_indices=True)
    return block_e, nused.reshape(1), slot_tok, dest


def kernel(x_prompt, x_sample, p_prompt, p_sample, cache_k, cache_v, cache_conv, g_mix, w_in, qn_g, kn_g, rel_bias, conv_w, conv_b, cln_g, cln_b, w_out, g_ffn, w_grp, b_grp, w_rt, b_rt, w_e_gate, w_e_up, w_e_down, g_ple, w_pg, b_pg, w_pe):
    depth = g_mix.shape[0]
    assert depth == 1, "single-layer step"
    B, S, D = x_prompt.shape
    Bs, Ss, _ = x_sample.shape
    Tp, Ts = B * S, Bs * Ss
    T_all = Tp + Ts
    keep_p = min(WINDOW, S)
    L_all = cache_k.shape[2] + Ss
    keep_s = min(WINDOW, L_all)
    assert S >= CONV_K - 1 and Ss >= CONV_K - 1 and L_all <= BAND

    TM = 256
    TQ = WINDOW - CHUNK
    TM_E = 512
    assert S % TQ == 0 and S % TM == 0 and Ts % TM == 0 and Tp % TM == 0

    i = 0
    w_in_b = w_in[i].astype(BF16)
    w_oa = w_out[i, :ATTN_WIDTH].astype(BF16)
    w_oc = w_out[i, ATTN_WIDTH:].astype(BF16)
    w_pg_b = w_pg[i].astype(BF16)
    w_pe_b = w_pe[i].astype(BF16)
    wg_b = w_e_gate[i].astype(BF16)
    wu_b = w_e_up[i].astype(BF16)
    wd_b = w_e_down[i].astype(BF16)
    row = lambda a: a.reshape(1, -1).astype(F32)
    gmix, gffn, gple, bpg = row(g_mix[i]), row(g_ffn[i]), row(g_ple[i]), row(b_pg[i])
    qg = row(jnp.tile(qn_g[i], N_HEADS))
    kg = row(jnp.tile(kn_g[i], N_HEADS))
    sgn = jnp.asarray(np.where(np.arange(LANES) < HEAD_DIM, 1.0, -1.0).reshape(1, LANES), F32)
    conv_w32 = jnp.pad(conv_w[i], ((0, HALO - CONV_K), (0, 0)))
    convb, clng, clnb = row(conv_b[i]), row(cln_g[i]), row(cln_b[i])
    w_r = jnp.pad(jnp.concatenate([w_grp[i], w_rt[i]], axis=1), ((0, 0), (0, ROUTE_W - N_GROUPS - N_EXPERTS)))
    wr_hi = w_r.astype(BF16)
    wr_lo = (w_r - wr_hi.astype(F32)).astype(BF16)
    b_r = jnp.pad(jnp.concatenate([b_grp[i], b_rt[i]]), (0, ROUTE_W - N_GROUPS - N_EXPERTS)).reshape(1, ROUTE_W)
    bias_p = _prompt_bias(rel_bias[i])
    bias_s = _sample_bias(rel_bias[i], Ss, L_all)

    nj = S // TM
    ntail = -(-keep_p // TM)
    tail_map = lambda t: ((t // nj) * ntail + jnp.maximum(t % nj - (nj - ntail), 0), 0)
    q_p, k_p, v_p, glu_p, kf_p, vf_p, gf_p = _in_proj(
        x_prompt.reshape(Tp, D), gmix, w_in_b, qg, kg, sgn, tm=TM,
        tail_rows=B * ntail * TM, kf_map=tail_map,
        gf_rows=HALO, gf_blocks=B, gf_map=lambda t: (t // nj, 0))
    cut = ntail * TM - keep_p
    new_k_p = kf_p.reshape(B, ntail * TM, N_HEADS, HEAD_DIM)[:, cut:][None]
    new_v_p = vf_p.reshape(B, ntail * TM, N_HEADS, HEAD_DIM)[:, cut:][None]
    new_c_p = gf_p.reshape(B, HALO, CONV_CH)[:, HALO - (CONV_K - 1):][None]

    conv_p = _conv_module(glu_p.reshape(B, S, CONV_CH), jnp.zeros((B, HALO, CONV_CH), F32),
                          conv_w32, convb, clng, clnb, ts=TM)
    attn_p = _attn_prompt(q_p, k_p, v_p, bias_p, B=B, S=S, tq=TQ)

    ident = lambda t: (t, 0)
    q_s, k_s, v_s, glu_s, kf_s, vf_s, gf_s = _in_proj(
        x_sample.reshape(Ts, D), gmix, w_in_b, qg, kg, sgn, tm=TM,
        tail_rows=Ts, kf_map=ident, gf_rows=TM, gf_blocks=Ts // TM, gf_map=ident)
    ck = cache_k[i].reshape(Bs, -1, ATTN_WIDTH)
    cv = cache_v[i].reshape(Bs, -1, ATTN_WIDTH)
    k_all = jnp.concatenate([ck, kf_s.reshape(Bs, Ss, ATTN_WIDTH)], axis=1)
    v_all = jnp.concatenate([cv, vf_s.reshape(Bs, Ss, ATTN_WIDTH)], axis=1)
    new_k_s = k_all[:, L_all - keep_s:].reshape(Bs, keep_s, N_HEADS, HEAD_DIM)[None]
    new_v_s = v_all[:, L_all - keep_s:].reshape(Bs, keep_s, N_HEADS, HEAD_DIM)[None]
    conv_in_s = jnp.concatenate([cache_conv[i], gf_s.reshape(Bs, Ss, CONV_CH)], axis=1)
    new_c_s = conv_in_s[:, -(CONV_K - 1):][None]
    padk = ((0, 0), (0, BAND - L_all), (0, 0))
    attn_s = _attn_sample(q_s.reshape(Bs, Ss, ATTN_WIDTH), jnp.pad(k_all.astype(BF16), padk),
                          jnp.pad(v_all.astype(BF16), padk), bias_s)
    halo_s = jnp.pad(cache_conv[i], ((0, 0), (HALO - (CONV_K - 1), 0), (0, 0)))
    conv_s = _conv_module(glu_s.reshape(Bs, Ss, CONV_CH), halo_s, conv_w32, convb, clng, clnb, ts=Ss)

    xm_all, route_all = _out_proj(attn_p, attn_s.reshape(Ts, ATTN_WIDTH),
                                  conv_p.reshape(Tp, CONV_CH), conv_s.reshape(Ts, CONV_CH),
                                  x_prompt.reshape(Tp, D), x_sample.reshape(Ts, D),
                                  w_oa, w_oc, gffn, wr_hi, wr_lo, b_r, tm=TM)

    nb = -(-2 * T_all // TM_E) + N_EXPERTS
    block_e, nused, slot_tok, dest = _routing_tables(route_all, tm=TM_E, nb=nb)
    moe_out = _moe(block_e, nused, slot_tok, xm_all, gffn, wg_b, wu_b, wd_b, tm=TM_E, nb=nb)

    y_p = _final(dest, xm_all, route_all, p_prompt[i].reshape(Tp, PLE_DIM), moe_out,
                 gple, w_pg_b, bpg, w_pe_b, tm=TM, blk0=0)
    y_s = _final(dest, xm_all, route_all, p_sample[i].reshape(Ts, PLE_DIM), moe_out,
                 gple, w_pg_b, bpg, w_pe_b, tm=TM, blk0=Tp // TM)
    return (y_p.reshape(B, S, D), y_s.reshape(Bs, Ss, D), new_k_p, new_v_p, new_c_p,
            new_k_s, new_v_s, new_c_s)
```

```python
import functools

import numpy as np
import jax
import jax.numpy as jnp
from jax import lax
from jax.experimental import pallas as pl
from jax.experimental.pallas import tpu as pltpu

F32 = jnp.float32
BF16 = jnp.bfloat16

D_MODEL = 2048
ATTN_WIDTH = 1024
CONV_CH = 1024
HEAD_DIM = 64
N_HEADS = 16
N_PAIRS = N_HEADS // 2
IN_WIDTH = 3 * ATTN_WIDTH + 2 * CONV_CH
CHUNK = 64
N_PREV_CHUNKS = 8
WINDOW = (N_PREV_CHUNKS + 1) * CHUNK
BAND = WINDOW + CHUNK
CONV_K = 31
HALO = 32
REL_CLIP = 128
PAST_LEN = 4096
N_GROUPS = 4
EXPERTS_PER_GROUP = 8
N_EXPERTS = 32
D_EXPERT = 1024
PLE_DIM = 256
EPS = 1e-6
NEG_INF = -1e30
LANES = 128
ROUTE_W = 128

VMEM_LIMIT = 56 * 1024 * 1024


DMA_UNROLL = 8


def _cparams(sem, row_dma=False):
    return pltpu.CompilerParams(dimension_semantics=sem, vmem_limit_bytes=VMEM_LIMIT,
                                disable_bounds_checks=row_dma)


def _rms(x, g):
    return x * lax.rsqrt(jnp.mean(x * x, axis=-1, keepdims=True) + EPS) * g


def _in_proj_body(x_ref, gmix_ref, w_ref, qg_ref, kg_ref, sgn_ref,
                  q_ref, k_ref, v_ref, glu_ref, kf_ref, vf_ref, gf_ref):
    h = _rms(x_ref[...], gmix_ref[...]).astype(BF16)
    sgn = sgn_ref[...]

    def seg(i):
        return jnp.dot(h, w_ref[:, i * ATTN_WIDTH:(i + 1) * ATTN_WIDTH], preferred_element_type=F32)

    def head_norm(u, g_ref, p):
        up = u[:, p * LANES:(p + 1) * LANES]
        sq = up * up
        s = jnp.sum(sq, axis=-1, keepdims=True)
        d = jnp.sum(sq * sgn, axis=-1, keepdims=True)
        ms = (s + sgn * d) * (0.5 / HEAD_DIM)
        return up * lax.rsqrt(ms + EPS) * g_ref[:, p * LANES:(p + 1) * LANES]

    uq = seg(0)
    for p in range(N_PAIRS):
        q_ref[:, p * LANES:(p + 1) * LANES] = (head_norm(uq, qg_ref, p) * (HEAD_DIM ** -0.5)).astype(BF16)
    uk = seg(1)
    for p in range(N_PAIRS):
        kn = head_norm(uk, kg_ref, p)
        kf_ref[:, p * LANES:(p + 1) * LANES] = kn
        k_ref[:, p * LANES:(p + 1) * LANES] = kn.astype(BF16)
    uv = seg(2)
    vf_ref[...] = uv
    v_ref[...] = uv.astype(BF16)
    glu = seg(3) * jax.nn.sigmoid(seg(4))
    glu_ref[...] = glu.astype(BF16)
    gr = gf_ref.shape[0]
    gf_ref[...] = glu[glu.shape[0] - gr:, :]


def _in_proj(x2d, gmix, w_in, qg, kg, sgn, *, tm, tail_rows, kf_map, gf_rows, gf_blocks, gf_map):
    T = x2d.shape[0]
    nt = T // tm
    row = lambda i: (i, 0)
    const = lambda i: (0, 0)
    out_shape = (
        jax.ShapeDtypeStruct((T, ATTN_WIDTH), BF16),
        jax.ShapeDtypeStruct((T, ATTN_WIDTH), BF16),
        jax.ShapeDtypeStruct((T, ATTN_WIDTH), BF16),
        jax.ShapeDtypeStruct((T, CONV_CH), BF16),
        jax.ShapeDtypeStruct((tail_rows, ATTN_WIDTH), F32),
        jax.ShapeDtypeStruct((tail_rows, ATTN_WIDTH), F32),
        jax.ShapeDtypeStruct((gf_blocks * gf_rows, CONV_CH), F32),
    )
    return pl.pallas_call(
        _in_proj_body,
        grid=(nt,),
        in_specs=[
            pl.BlockSpec((tm, D_MODEL), row),
            pl.BlockSpec((1, D_MODEL), const),
            pl.BlockSpec((D_MODEL, IN_WIDTH), const, pipeline_mode=pl.Buffered(1)),
            pl.BlockSpec((1, ATTN_WIDTH), const),
            pl.BlockSpec((1, ATTN_WIDTH), const),
            pl.BlockSpec((1, LANES), const),
        ],
        out_specs=(
            pl.BlockSpec((tm, ATTN_WIDTH), row),
            pl.BlockSpec((tm, ATTN_WIDTH), row),
            pl.BlockSpec((tm, ATTN_WIDTH), row),
            pl.BlockSpec((tm, CONV_CH), row),
            pl.BlockSpec((tm, ATTN_WIDTH), kf_map),
            pl.BlockSpec((tm, ATTN_WIDTH), kf_map),
            pl.BlockSpec((gf_rows, CONV_CH), gf_map),
        ),
        out_shape=out_shape,
        compiler_params=_cparams(("arbitrary",)),
        name="in_proj",
    )(x2d, gmix, w_in, qg, kg, sgn)


CONV_ROWS = 64


def _conv_body(g_ref, gh_ref, h0_ref, w_ref, b_ref, lg_ref, lb_ref, o_ref, xc_ref, y_ref):
    j = pl.program_id(1)
    ts = g_ref.shape[1]
    first = j == 0
    xc_ref[0:HALO, :] = jnp.where(first, h0_ref[0], gh_ref[0].astype(F32))
    xc_ref[HALO:, :] = g_ref[0].astype(F32)
    off = HALO - (CONV_K - 1)
    rc = min(CONV_ROWS, ts)
    for p in range(CONV_CH // LANES):
        lanes = slice(p * LANES, (p + 1) * LANES)
        wp = w_ref[:, lanes]
        bp = b_ref[:, lanes]
        for r in range(ts // rc):
            r0 = r * rc
            acc = jnp.broadcast_to(bp, (rc, LANES))
            for t in range(CONV_K):
                acc = acc + wp[t:t + 1, :] * xc_ref[r0 + off + t:r0 + off + t + rc, lanes]
            y_ref[r0:r0 + rc, lanes] = acc
    y = y_ref[...]
    mu = jnp.mean(y, axis=-1, keepdims=True)
    yc = y - mu
    z = yc * lax.rsqrt(jnp.mean(yc * yc, axis=-1, keepdims=True) + EPS) * lg_ref[...] + lb_ref[...]
    o_ref[0] = (z * jax.nn.sigmoid(z)).astype(BF16)


def _conv_module(glu3, halo0, conv_w32, conv_b, cln_g, cln_b, *, ts):
    B, S, C = glu3.shape
    nj = S // ts
    hb = ts // HALO
    const = lambda b, j: (0, 0)
    return pl.pallas_call(
        _conv_body,
        grid=(B, nj),
        in_specs=[
            pl.BlockSpec((1, ts, C), lambda b, j: (b, j, 0)),
            pl.BlockSpec((1, HALO, C), lambda b, j: (b, jnp.maximum(j * hb - 1, 0), 0)),
            pl.BlockSpec((1, HALO, C), lambda b, j: (b, 0, 0)),
            pl.BlockSpec((HALO, C), const),
            pl.BlockSpec((1, C), const),
            pl.BlockSpec((1, C), const),
            pl.BlockSpec((1, C), const),
        ],
        out_specs=pl.BlockSpec((1, ts, C), lambda b, j: (b, j, 0)),
        out_shape=jax.ShapeDtypeStruct((B, S, C), BF16),
        scratch_shapes=[pltpu.VMEM((ts + HALO, C), F32), pltpu.VMEM((ts, C), F32)],
        compiler_params=_cparams(("arbitrary", "arbitrary")),
        name="conv_module",
    )(glu3, glu3, halo0, conv_w32, conv_b, cln_g, cln_b)


def _pair_attend(qp, kb, vb, bias, first_valid):
    R = qp.shape[0]
    lane = lax.broadcasted_iota(jnp.int32, (R, LANES), 1)
    zero = jnp.zeros_like(qp)
    qq = jnp.concatenate([jnp.where(lane < HEAD_DIM, qp, zero),
                          jnp.where(lane >= HEAD_DIM, qp, zero)], axis=0)
    s = lax.dot_general(qq, kb, (((1,), (1,)), ((), ())), preferred_element_type=F32)
    s = s + bias
    if first_valid is not None:
        kidx = lax.broadcasted_iota(jnp.int32, s.shape, 1)
        s = jnp.where(kidx >= first_valid, s, NEG_INF)
    m = jnp.max(s, axis=-1, keepdims=True)
    e = jnp.exp(s - m)
    l = jnp.sum(e, axis=-1, keepdims=True)
    o2 = jnp.dot(e.astype(BF16), vb, preferred_element_type=F32) * (1.0 / l)
    return jnp.where(lane < HEAD_DIM, o2[:R], o2[R:])


def _attn_prompt_body(q_ref, kp_ref, kc_ref, vp_ref, vc_ref, bias_ref, o_ref, kcat, vcat):
    j = pl.program_id(1)
    tq = q_ref.shape[0]
    nch = tq // CHUNK
    kcat[0:CHUNK, :] = jnp.zeros((CHUNK, ATTN_WIDTH), BF16)
    vcat[0:CHUNK, :] = jnp.zeros((CHUNK, ATTN_WIDTH), BF16)
    kcat[CHUNK:CHUNK + tq, :] = kp_ref[...]
    vcat[CHUNK:CHUNK + tq, :] = vp_ref[...]
    kcat[CHUNK + tq:, :] = kc_ref[...]
    vcat[CHUNK + tq:, :] = vc_ref[...]
    hist = jnp.where(j == 0, 0, tq)

    def chunk(c, carry):
        r0 = pl.multiple_of(c * CHUNK, CHUNK)
        first_valid = CHUNK + tq - hist - c * CHUNK
        for p in range(N_PAIRS):
            lanes = slice(p * LANES, (p + 1) * LANES)
            o = _pair_attend(q_ref[pl.ds(r0, CHUNK), lanes],
                             kcat[pl.ds(r0, BAND), lanes],
                             vcat[pl.ds(r0, BAND), lanes],
                             bias_ref[p], first_valid)
            o_ref[pl.ds(r0, CHUNK), lanes] = o.astype(BF16)
        return carry

    lax.fori_loop(0, nch, chunk, 0)


def _attn_prompt(q, k, v, bias, *, B, S, tq):
    assert tq == WINDOW - CHUNK, "key band of a query tile = previous tile + own tile"
    nj = S // tq
    cur = lambda b, j: (b * nj + j, 0)
    prev = lambda b, j: (b * nj + jnp.maximum(j - 1, 0), 0)
    blk = (tq, ATTN_WIDTH)
    return pl.pallas_call(
        _attn_prompt_body,
        grid=(B, nj),
        in_specs=[
            pl.BlockSpec(blk, cur),
            pl.BlockSpec(blk, prev),
            pl.BlockSpec(blk, cur),
            pl.BlockSpec(blk, prev),
            pl.BlockSpec(blk, cur),
            pl.BlockSpec((N_PAIRS, 2 * CHUNK, BAND), lambda b, j: (0, 0, 0)),
        ],
        out_specs=pl.BlockSpec(blk, cur),
        out_shape=jax.ShapeDtypeStruct((B * S, ATTN_WIDTH), BF16),
        scratch_shapes=[pltpu.VMEM((CHUNK + 2 * tq, ATTN_WIDTH), BF16),
                        pltpu.VMEM((CHUNK + 2 * tq, ATTN_WIDTH), BF16)],
        compiler_params=_cparams(("arbitrary", "arbitrary")),
        name="attn_prompt",
    )(q, k, k, v, v, bias)


def _attn_sample_body(q_ref, k_ref, v_ref, bias_ref, o_ref):
    for p in range(N_PAIRS):
        lanes = slice(p * LANES, (p + 1) * LANES)
        o = _pair_attend(q_ref[0, :, lanes], k_ref[0, :, lanes], v_ref[0, :, lanes], bias_ref[p], None)
        o_ref[0, :, lanes] = o.astype(BF16)


def _attn_sample(q3, k3, v3, bias):
    B, R, _ = q3.shape
    return pl.pallas_call(
        _attn_sample_body,
        grid=(B,),
        in_specs=[
            pl.BlockSpec((1, R, ATTN_WIDTH), lambda b: (b, 0, 0)),
            pl.BlockSpec((1, BAND, ATTN_WIDTH), lambda b: (b, 0, 0)),
            pl.BlockSpec((1, BAND, ATTN_WIDTH), lambda b: (b, 0, 0)),
            pl.BlockSpec((N_PAIRS, 2 * R, BAND), lambda b: (0, 0, 0)),
        ],
        out_specs=pl.BlockSpec((1, R, ATTN_WIDTH), lambda b: (b, 0, 0)),
        out_shape=jax.ShapeDtypeStruct((B, R, ATTN_WIDTH), BF16),
        compiler_params=_cparams(("arbitrary",)),
        name="attn_sample",
    )(q3, k3, v3, bias)


def _split_bf16(x):
    hi = x.astype(BF16)
    lo = (x - hi.astype(F32)).astype(BF16)
    return hi, lo


def _out_proj_body(ap_ref, as_ref, cp_ref, cs_ref, xp_ref, xs_ref, wa_ref, wc_ref, gffn_ref,
                   wrh_ref, wrl_ref, br_ref, xm_ref, route_ref, *, ntp):
    is_p = pl.program_id(0) < ntp
    a = jnp.where(is_p, ap_ref[...], as_ref[...])
    c = jnp.where(is_p, cp_ref[...], cs_ref[...])
    x = jnp.where(is_p, xp_ref[...], xs_ref[...])
    mix = jnp.dot(a, wa_ref[...], preferred_element_type=F32)
    mix = mix + jnp.dot(c, wc_ref[...], preferred_element_type=F32)
    xm = x + mix
    xm_ref[...] = xm
    h = _rms(xm, gffn_ref[...])
    hh, hl = _split_bf16(h)
    wh = wrh_ref[...]
    lg = (jnp.dot(hh, wh, preferred_element_type=F32)
          + jnp.dot(hl, wh, preferred_element_type=F32)
          + jnp.dot(hh, wrl_ref[...], preferred_element_type=F32)) + br_ref[...]
    lane = lax.broadcasted_iota(jnp.int32, lg.shape, 1)
    big = jnp.int32(1 << 20)

    def top1(mask):
        v = jnp.max(jnp.where(mask, lg, -jnp.inf), axis=-1, keepdims=True)
        i = jnp.min(jnp.where(mask & (lg == v), lane, big), axis=-1, keepdims=True)
        return v, i

    gmask = lane < N_GROUPS
    gv, gi = top1(gmask)
    p_sel = 1.0 / jnp.sum(jnp.where(gmask, jnp.exp(lg - gv), 0.0), axis=-1, keepdims=True)
    lo_lane = N_GROUPS + gi * EXPERTS_PER_GROUP
    emask = (lane >= lo_lane) & (lane < lo_lane + EXPERTS_PER_GROUP)
    v1, i1 = top1(emask)
    v2, i2 = top1(emask & (lane != i1))
    e2 = jnp.exp(v2 - v1)
    g1 = p_sel / (1.0 + e2)
    g2 = p_sel * e2 / (1.0 + e2)
    route = jnp.where(lane == 0, (i1 - N_GROUPS).astype(F32),
                      jnp.where(lane == 1, (i2 - N_GROUPS).astype(F32),
                                jnp.where(lane == 2, g1, jnp.where(lane == 3, g2, 0.0))))
    route_ref[...] = route


def _out_proj(attn_p, attn_s, conv_p, conv_s, x_p, x_s, w_oa, w_oc, gffn, wr_hi, wr_lo, b_r, *, tm):
    Tp, Ts = x_p.shape[0], x_s.shape[0]
    ntp, nts = Tp // tm, Ts // tm
    pmap = lambda i: (jnp.minimum(i, ntp - 1), 0)
    smap = lambda i: (jnp.maximum(i - ntp, 0), 0)
    row = lambda i: (i, 0)
    const = lambda i: (0, 0)
    return pl.pallas_call(
        functools.partial(_out_proj_body, ntp=ntp),
        grid=(ntp + nts,),
        in_specs=[
            pl.BlockSpec((tm, ATTN_WIDTH), pmap),
            pl.BlockSpec((tm, ATTN_WIDTH), smap),
            pl.BlockSpec((tm, CONV_CH), pmap),
            pl.BlockSpec((tm, CONV_CH), smap),
            pl.BlockSpec((tm, D_MODEL), pmap),
            pl.BlockSpec((tm, D_MODEL), smap),
            pl.BlockSpec((ATTN_WIDTH, D_MODEL), const),
            pl.BlockSpec((CONV_CH, D_MODEL), const),
            pl.BlockSpec((1, D_MODEL), const),
            pl.BlockSpec((D_MODEL, ROUTE_W), const),
            pl.BlockSpec((D_MODEL, ROUTE_W), const),
            pl.BlockSpec((1, ROUTE_W), const),
        ],
        out_specs=(pl.BlockSpec((tm, D_MODEL), row), pl.BlockSpec((tm, ROUTE_W), row)),
        out_shape=(jax.ShapeDtypeStruct((Tp + Ts, D_MODEL), F32),
                   jax.ShapeDtypeStruct((Tp + Ts, ROUTE_W), F32)),
        compiler_params=_cparams(("arbitrary",)),
        name="out_proj",
    )(attn_p, attn_s, conv_p, conv_s, x_p, x_s, w_oa, w_oc, gffn, wr_hi, wr_lo, b_r)


def _moe_body(be_ref, nu_ref, tok_ref, xm_hbm, gffn_ref, wg_ref, wu_ref, wd_ref, out_ref, xbuf, sem):
    b = pl.program_id(0)
    tm = out_ref.shape[0]
    nused = nu_ref[0]

    def issue(blk, slot):
        def body(r, carry):
            tok = tok_ref[blk * tm + r]
            pltpu.make_async_copy(xm_hbm.at[pl.ds(tok, 1)], xbuf.at[slot, pl.ds(r, 1)], sem.at[slot]).start()
            return carry
        lax.fori_loop(0, tm, body, 0, unroll=DMA_UNROLL)

    @pl.when(b == 0)
    def _():
        issue(0, 0)

    @pl.when(b < nused)
    def _():
        slot = b % 2

        @pl.when(b + 1 < nused)
        def _():
            issue(b + 1, 1 - slot)

        pltpu.make_async_copy(xm_hbm.at[pl.ds(0, tm)], xbuf.at[slot], sem.at[slot]).wait()
        h = _rms(xbuf[slot], gffn_ref[...]).astype(BF16)
        g = jnp.dot(h, wg_ref[0], preferred_element_type=F32)
        u = jnp.dot(h, wu_ref[0], preferred_element_type=F32)
        a = (g * jax.nn.sigmoid(g) * u).astype(BF16)
        out_ref[...] = jnp.dot(a, wd_ref[0], preferred_element_type=F32)

    @pl.when(b >= nused)
    def _():
        out_ref[...] = jnp.zeros(out_ref.shape, F32)


def _moe(block_e, nused, slot_tok, xm_all, gffn, wg, wu, wd, *, tm, nb):
    wmap = lambda b, be, nu, tok: (be[b], 0, 0)
    grid_spec = pltpu.PrefetchScalarGridSpec(
        num_scalar_prefetch=3,
        grid=(nb,),
        in_specs=[
            pl.BlockSpec(memory_space=pl.ANY),
            pl.BlockSpec((1, D_MODEL), lambda b, be, nu, tok: (0, 0)),
            pl.BlockSpec((1, D_MODEL, D_EXPERT), wmap),
            pl.BlockSpec((1, D_MODEL, D_EXPERT), wmap),
            pl.BlockSpec((1, D_EXPERT, D_MODEL), wmap),
        ],
        out_specs=pl.BlockSpec((tm, D_MODEL), lambda b, be, nu, tok: (b, 0)),
        scratch_shapes=[pltpu.VMEM((2, tm, D_MODEL), F32), pltpu.SemaphoreType.DMA((2,))],
    )
    return pl.pallas_call(
        _moe_body,
        grid_spec=grid_spec,
        out_shape=jax.ShapeDtypeStruct((nb * tm, D_MODEL), F32),
        compiler_params=_cparams(("arbitrary",), row_dma=True),
        name="moe",
    )(block_e, nused, slot_tok, xm_all, gffn, wg, wu, wd)


def _final_body(dest_ref, xm_ref, route_ref, p_ref, mo_hbm, gple_ref, wpg_ref, bpg_ref, wpe_ref,
                y_ref, ybuf, sem, *, tok0):
    i = pl.program_id(0)
    nt = pl.num_programs(0)
    tm = y_ref.shape[0]

    def issue(tile, slot):
        def body(r, carry):
            base = 2 * (tok0 + tile * tm + r)
            for kk in range(2):
                d = dest_ref[base + kk]
                pltpu.make_async_copy(mo_hbm.at[pl.ds(d, 1)], ybuf.at[slot, kk, pl.ds(r, 1)], sem.at[slot]).start()
            return carry
        lax.fori_loop(0, tm, body, 0, unroll=DMA_UNROLL)

    @pl.when(i == 0)
    def _():
        issue(0, 0)

    slot = i % 2

    @pl.when(i + 1 < nt)
    def _():
        issue(i + 1, 1 - slot)

    for kk in range(2):
        pltpu.make_async_copy(mo_hbm.at[pl.ds(0, tm)], ybuf.at[slot, kk], sem.at[slot]).wait()
    route = route_ref[...]
    x2 = xm_ref[...] + route[:, 2:3] * ybuf[slot, 0] + route[:, 3:4] * ybuf[slot, 1]
    r = _rms(x2, gple_ref[...]).astype(BF16)
    z = jnp.dot(r, wpg_ref[...], preferred_element_type=F32) + bpg_ref[...]
    pe = jnp.dot(p_ref[...].astype(BF16), wpe_ref[...], preferred_element_type=F32)
    y_ref[...] = x2 + jax.nn.sigmoid(z) * pe


def _final(dest, xm_all, route, p2d, moe_out, gple, wpg, bpg, wpe, *, tm, blk0):
    T = p2d.shape[0]
    nt = T // tm
    const = lambda i, d: (0, 0)
    grid_spec = pltpu.PrefetchScalarGridSpec(
        num_scalar_prefetch=1,
        grid=(nt,),
        in_specs=[
            pl.BlockSpec((tm, D_MODEL), lambda i, d: (i + blk0, 0)),
            pl.BlockSpec((tm, ROUTE_W), lambda i, d: (i + blk0, 0)),
            pl.BlockSpec((tm, PLE_DIM), lambda i, d: (i, 0)),
            pl.BlockSpec(memory_space=pl.ANY),
            pl.BlockSpec((1, D_MODEL), const),
            pl.BlockSpec((D_MODEL, D_MODEL), const),
            pl.BlockSpec((1, D_MODEL), const),
            pl.BlockSpec((PLE_DIM, D_MODEL), const),
        ],
        out_specs=pl.BlockSpec((tm, D_MODEL), lambda i, d: (i, 0)),
        scratch_shapes=[pltpu.VMEM((2, 2, tm, D_MODEL), F32), pltpu.SemaphoreType.DMA((2,))],
    )
    return pl.pallas_call(
        functools.partial(_final_body, tok0=blk0 * tm),
        grid_spec=grid_spec,
        out_shape=jax.ShapeDtypeStruct((T, D_MODEL), F32),
        compiler_params=_cparams(("arbitrary",), row_dma=True),
        name="final",
    )(dest, xm_all, route, p2d, moe_out, gple, wpg, bpg, wpe)


def _pair_stack(bias):
    H, R, K = bias.shape
    return bias.reshape(H // 2, 2 * R, K)


def _prompt_bias(rel_bias):
    off_q = np.arange(CHUNK)
    off_k = np.arange(BAND) - (BAND - CHUNK)
    idx = np.clip(off_q[:, None] - off_k[None, :], -REL_CLIP, REL_CLIP) + REL_CLIP
    tbl = rel_bias[:, idx]
    tbl = jnp.where((np.arange(BAND) >= CHUNK)[None, None, :], tbl, NEG_INF)
    return _pair_stack(tbl.astype(F32))


def _sample_bias(rel_bias, S, L_all):
    q_pos = PAST_LEN + np.arange(S)
    k_pos = PAST_LEN + S - L_all + np.arange(BAND)
    idx = np.clip(q_pos[:, None] - k_pos[None, :], -REL_CLIP, REL_CLIP) + REL_CLIP
    dc = q_pos[:, None] // CHUNK - k_pos[None, :] // CHUNK
    mask = (dc >= 0) & (dc <= N_PREV_CHUNKS) & (np.arange(BAND) < L_all)[None, :]
    tbl = jnp.where(mask[None], rel_bias[:, idx], NEG_INF)
    return _pair_stack(tbl.astype(F32))


def _routing_tables(route_all, *, tm, nb):
    T_all = route_all.shape[0]
    A = 2 * T_all
    e_flat = route_all[:, 0:2].astype(jnp.int32).reshape(A)
    onehot = e_flat[:, None] == jnp.arange(N_EXPERTS, dtype=jnp.int32)[None, :]
    ch = LANES
    assert A % ch == 0 and A < (1 << 24)
    oh3 = onehot.astype(F32).reshape(A // ch, ch, N_EXPERTS)
    tri = jnp.asarray(np.tril(np.ones((ch, ch), np.float32)))
    within = jnp.einsum('ij,cjk->cik', tri, oh3)
    tot = within[:, -1, :]
    base = jnp.cumsum(tot, axis=0) - tot
    counts = (base[-1] + tot[-1]).astype(jnp.int32)
    pcounts = (counts + tm - 1) // tm * tm
    pends = jnp.cumsum(pcounts)
    pstarts = pends - pcounts
    slot = within + base[:, None, :] - 1.0 + pstarts.astype(F32)[None, None, :]
    dest = jnp.sum(jnp.where(oh3 > 0, slot, 0.0), axis=-1).astype(jnp.int32).reshape(A)
    nused = (pends[-1] // tm).astype(jnp.int32)
    blk = jnp.minimum(jnp.arange(nb, dtype=jnp.int32), nused - 1)
    block_e = jnp.minimum(jnp.sum((pends[None, :] <= (blk * tm)[:, None]).astype(jnp.int32), axis=1),
                          N_EXPERTS - 1).astype(jnp.int32)
    slot_tok = jnp.zeros((nb * tm,), jnp.int32).at[dest].set(
        jnp.arange(A, dtype=jnp.int32) // 2, unique_indices=True)
    return block_e, nused.reshape(1), slot_tok, dest


def kernel(x_prompt, x_sample, p_prompt, p_sample, cache_k, cache_v, cache_conv, g_mix, w_in, qn_g, kn_g, rel_bias, conv_w, conv_b, cln_g, cln_b, w_out, g_ffn, w_grp, b_grp, w_rt, b_rt, w_e_gate, w_e_up, w_e_down, g_ple, w_pg, b_pg, w_pe):
    depth = g_mix.shape[0]
    assert depth == 1, "single-layer step"
    B, S, D = x_prompt.shape
    Bs, Ss, _ = x_sample.shape
    Tp, Ts = B * S, Bs * Ss
    T_all = Tp + Ts
    keep_p = min(WINDOW, S)
    L_all = cache_k.shape[2] + Ss
    keep_s = min(WINDOW, L_all)
    assert S >= CONV_K - 1 and Ss >= CONV_K - 1 and L_all <= BAND

    TM = 256
    TQ = WINDOW - CHUNK
    TM_E = 512
    assert S % TQ == 0 and S % TM == 0 and Ts % TM == 0 and Tp % TM == 0

    i = 0
    w_in_b = w_in[i].astype(BF16)
    w_oa = w_out[i, :ATTN_WIDTH].astype(BF16)
    w_oc = w_out[i, ATTN_WIDTH:].astype(BF16)
    w_pg_b = w_pg[i].astype(BF16)
    w_pe_b = w_pe[i].astype(BF16)
    wg_b = w_e_gate[i].astype(BF16)
    wu_b = w_e_up[i].astype(BF16)
    wd_b = w_e_down[i].astype(BF16)
    row = lambda a: a.reshape(1, -1).astype(F32)
    gmix, gffn, gple, bpg = row(g_mix[i]), row(g_ffn[i]), row(g_ple[i]), row(b_pg[i])
    qg = row(jnp.tile(qn_g[i], N_HEADS))
    kg = row(jnp.tile(kn_g[i], N_HEADS))
    sgn = jnp.asarray(np.where(np.arange(LANES) < HEAD_DIM, 1.0, -1.0).reshape(1, LANES), F32)
    conv_w32 = jnp.pad(conv_w[i], ((0, HALO - CONV_K), (0, 0)))
    convb, clng, clnb = row(conv_b[i]), row(cln_g[i]), row(cln_b[i])
    w_r = jnp.pad(jnp.concatenate([w_grp[i], w_rt[i]], axis=1), ((0, 0), (0, ROUTE_W - N_GROUPS - N_EXPERTS)))
    wr_hi = w_r.astype(BF16)
    wr_lo = (w_r - wr_hi.astype(F32)).astype(BF16)
    b_r = jnp.pad(jnp.concatenate([b_grp[i], b_rt[i]]), (0, ROUTE_W - N_GROUPS - N_EXPERTS)).reshape(1, ROUTE_W)
    bias_p = _prompt_bias(rel_bias[i])
    bias_s = _sample_bias(rel_bias[i], Ss, L_all)

    nj = S // TM
    ntail = -(-keep_p // TM)
    tail_map = lambda t: ((t // nj) * ntail + jnp.maximum(t % nj - (nj - ntail), 0), 0)
    q_p, k_p, v_p, glu_p, kf_p, vf_p, gf_p = _in_proj(
        x_prompt.reshape(Tp, D), gmix, w_in_b, qg, kg, sgn, tm=TM,
        tail_rows=B * ntail * TM, kf_map=tail_map,
        gf_rows=HALO, gf_blocks=B, gf_map=lambda t: (t // nj, 0))
    cut = ntail * TM - keep_p
    new_k_p = kf_p.reshape(B, ntail * TM, N_HEADS, HEAD_DIM)[:, cut:][None]
    new_v_p = vf_p.reshape(B, ntail * TM, N_HEADS, HEAD_DIM)[:, cut:][None]
    new_c_p = gf_p.reshape(B, HALO, CONV_CH)[:, HALO - (CONV_K - 1):][None]

    conv_p = _conv_module(glu_p.reshape(B, S, CONV_CH), jnp.zeros((B, HALO, CONV_CH), F32),
                          conv_w32, convb, clng, clnb, ts=TM)
    attn_p = _attn_prompt(q_p, k_p, v_p, bias_p, B=B, S=S, tq=TQ)

    ident = lambda t: (t, 0)
    q_s, k_s, v_s, glu_s, kf_s, vf_s, gf_s = _in_proj(
        x_sample.reshape(Ts, D), gmix, w_in_b, qg, kg, sgn, tm=TM,
        tail_rows=Ts, kf_map=ident, gf_rows=TM, gf_blocks=Ts // TM, gf_map=ident)
    ck = cache_k[i].reshape(Bs, -1, ATTN_WIDTH)
    cv = cache_v[i].reshape(Bs, -1, ATTN_WIDTH)
    k_all = jnp.concatenate([ck, kf_s.reshape(Bs, Ss, ATTN_WIDTH)], axis=1)
    v_all = jnp.concatenate([cv, vf_s.reshape(Bs, Ss, ATTN_WIDTH)], axis=1)
    new_k_s = k_all[:, L_all - keep_s:].reshape(Bs, keep_s, N_HEADS, HEAD_DIM)[None]
    new_v_s = v_all[:, L_all - keep_s:].reshape(Bs, keep_s, N_HEADS, HEAD_DIM)[None]
    conv_in_s = jnp.concatenate([cache_conv[i], gf_s.reshape(Bs, Ss, CONV_CH)], axis=1)
    new_c_s = conv_in_s[:, -(CONV_K - 1):][None]
    padk = ((0, 0), (0, BAND - L_all), (0, 0))
    attn_s = _attn_sample(q_s.reshape(Bs, Ss, ATTN_WIDTH), jnp.pad(k_all.astype(BF16), padk),
                          jnp.pad(v_all.astype(BF16), padk), bias_s)
    halo_s = jnp.pad(cache_conv[i], ((0, 0), (HALO - (CONV_K - 1), 0), (0, 0)))
    conv_s = _conv_module(glu_s.reshape(Bs, Ss, CONV_CH), halo_s, conv_w32, convb, clng, clnb, ts=Ss)

    xm_all, route_all = _out_proj(attn_p, attn_s.reshape(Ts, ATTN_WIDTH),
                                  conv_p.reshape(Tp, CONV_CH), conv_s.reshape(Ts, CONV_CH),
                                  x_prompt.reshape(Tp, D), x_sample.reshape(Ts, D),
                                  w_oa, w_oc, gffn, wr_hi, wr_lo, b_r, tm=TM)

    nb = -(-2 * T_all // TM_E) + N_EXPERTS
    block_e, nused, slot_tok, dest = _routing_tables(route_all, tm=TM_E, nb=nb)
    moe_out = _moe(block_e, nused, slot_tok, xm_all, gffn, wg_b, wu_b, wd_b, tm=TM_E, nb=nb)

    y_p = _final(dest, xm_all, route_all, p_prompt[i].reshape(Tp, PLE_DIM), moe_out,
                 gple, w_pg_b, bpg, w_pe_b, tm=TM, blk0=0)
    y_s = _final(dest, xm_all, route_all, p_sample[i].reshape(Ts, PLE_DIM), moe_out,
                 gple, w_pg_b, bpg, w_pe_b, tm=TM, blk0=Tp // TM)
    return (y_p.reshape(B, S, D), y_s.reshape(Bs, Ss, D), new_k_p, new_v_p, new_c_p,
            new_k_s, new_v_s, new_c_s)
```

```python
import functools

import numpy as np
import jax
import jax.numpy as jnp
from jax import lax
from jax.experimental import pallas as pl
from jax.experimental.pallas import tpu as pltpu

F32 = jnp.float32
BF16 = jnp.bfloat16

D_MODEL = 2048
ATTN_WIDTH = 1024
CONV_CH = 1024
HEAD_DIM = 64
N_HEADS = 16
N_PAIRS = N_HEADS // 2
IN_WIDTH = 3 * ATTN_WIDTH + 2 * CONV_CH
CHUNK = 64
N_PREV_CHUNKS = 8
WINDOW = (N_PREV_CHUNKS + 1) * CHUNK
BAND = WINDOW + CHUNK
CONV_K = 31
HALO = 32
REL_CLIP = 128
PAST_LEN = 4096
N_GROUPS = 4
EXPERTS_PER_GROUP = 8
N_EXPERTS = 32
D_EXPERT = 1024
PLE_DIM = 256
EPS = 1e-6
NEG_INF = -1e30
LANES = 128
ROUTE_W = 128

VMEM_LIMIT = 56 * 1024 * 1024


DMA_UNROLL = 8


def _cparams(sem, row_dma=False):
    return pltpu.CompilerParams(dimension_semantics=sem, vmem_limit_bytes=VMEM_LIMIT,
                                disable_bounds_checks=row_dma)


def _rms(x, g):
    return x * lax.rsqrt(jnp.mean(x * x, axis=-1, keepdims=True) + EPS) * g


def _in_proj_body(x_ref, gmix_ref, w_ref, qg_ref, kg_ref, sgn_ref,
                  q_ref, k_ref, v_ref, glu_ref, kf_ref, vf_ref, gf_ref):
    h = _rms(x_ref[...], gmix_ref[...]).astype(BF16)
    sgn = sgn_ref[...]

    def seg(i):
        return jnp.dot(h, w_ref[:, i * ATTN_WIDTH:(i + 1) * ATTN_WIDTH], preferred_element_type=F32)

    def head_norm(u, g_ref, p):
        up = u[:, p * LANES:(p + 1) * LANES]
        sq = up * up
        s = jnp.sum(sq, axis=-1, keepdims=True)
        d = jnp.sum(sq * sgn, axis=-1, keepdims=True)
        ms = (s + sgn * d) * (0.5 / HEAD_DIM)
        return up * lax.rsqrt(ms + EPS) * g_ref[:, p * LANES:(p + 1) * LANES]

    uq = seg(0)
    for p in range(N_PAIRS):
        q_ref[:, p * LANES:(p + 1) * LANES] = (head_norm(uq, qg_ref, p) * (HEAD_DIM ** -0.5)).astype(BF16)
    uk = seg(1)
    for p in range(N_PAIRS):
        kn = head_norm(uk, kg_ref, p)
        kf_ref[:, p * LANES:(p + 1) * LANES] = kn
        k_ref[:, p * LANES:(p + 1) * LANES] = kn.astype(BF16)
    uv = seg(2)
    vf_ref[...] = uv
    v_ref[...] = uv.astype(BF16)
    glu = seg(3) * jax.nn.sigmoid(seg(4))
    glu_ref[...] = glu.astype(BF16)
    gr = gf_ref.shape[0]
    gf_ref[...] = glu[glu.shape[0] - gr:, :]


def _in_proj(x2d, gmix, w_in, qg, kg, sgn, *, tm, tail_rows, kf_map, gf_rows, gf_blocks, gf_map):
    T = x2d.shape[0]
    nt = T // tm
    row = lambda i: (i, 0)
    const = lambda i: (0, 0)
    out_shape = (
        jax.ShapeDtypeStruct((T, ATTN_WIDTH), BF16),
        jax.ShapeDtypeStruct((T, ATTN_WIDTH), BF16),
        jax.ShapeDtypeStruct((T, ATTN_WIDTH), BF16),
        jax.ShapeDtypeStruct((T, CONV_CH), BF16),
        jax.ShapeDtypeStruct((tail_rows, ATTN_WIDTH), F32),
        jax.ShapeDtypeStruct((tail_rows, ATTN_WIDTH), F32),
        jax.ShapeDtypeStruct((gf_blocks * gf_rows, CONV_CH), F32),
    )
    return pl.pallas_call(
        _in_proj_body,
        grid=(nt,),
        in_specs=[
            pl.BlockSpec((tm, D_MODEL), row),
            pl.BlockSpec((1, D_MODEL), const),
            pl.BlockSpec((D_MODEL, IN_WIDTH), const, pipeline_mode=pl.Buffered(1)),
            pl.BlockSpec((1, ATTN_WIDTH), const),
            pl.BlockSpec((1, ATTN_WIDTH), const),
            pl.BlockSpec((1, LANES), const),
        ],
        out_specs=(
            pl.BlockSpec((tm, ATTN_WIDTH), row),
            pl.BlockSpec((tm, ATTN_WIDTH), row),
            pl.BlockSpec((tm, ATTN_WIDTH), row),
            pl.BlockSpec((tm, CONV_CH), row),
            pl.BlockSpec((tm, ATTN_WIDTH), kf_map),
            pl.BlockSpec((tm, ATTN_WIDTH), kf_map),
            pl.BlockSpec((gf_rows, CONV_CH), gf_map),
        ),
        out_shape=out_shape,
        compiler_params=_cparams(("arbitrary",)),
        name="in_proj",
    )(x2d, gmix, w_in, qg, kg, sgn)


CONV_ROWS = 64


def _conv_body(g_ref, gh_ref, h0_ref, w_ref, b_ref, lg_ref, lb_ref, o_ref, xc_ref, y_ref):
    j = pl.program_id(1)
    ts = g_ref.shape[1]
    first = j == 0
    xc_ref[0:HALO, :] = jnp.where(first, h0_ref[0], gh_ref[0].astype(F32))
    xc_ref[HALO:, :] = g_ref[0].astype(F32)
    off = HALO - (CONV_K - 1)
    rc = min(CONV_ROWS, ts)
    for p in range(CONV_CH // LANES):
        lanes = slice(p * LANES, (p + 1) * LANES)
        wp = w_ref[:, lanes]
        bp = b_ref[:, lanes]
        for r in range(ts // rc):
            r0 = r * rc
            acc = jnp.broadcast_to(bp, (rc, LANES))
            for t in range(CONV_K):
                acc = acc + wp[t:t + 1, :] * xc_ref[r0 + off + t:r0 + off + t + rc, lanes]
            y_ref[r0:r0 + rc, lanes] = acc
    y = y_ref[...]
    mu = jnp.mean(y, axis=-1, keepdims=True)
    yc = y - mu
    z = yc * lax.rsqrt(jnp.mean(yc * yc, axis=-1, keepdims=True) + EPS) * lg_ref[...] + lb_ref[...]
    o_ref[0] = (z * jax.nn.sigmoid(z)).astype(BF16)


def _conv_module(glu3, halo0, conv_w32, conv_b, cln_g, cln_b, *, ts):
    B, S, C = glu3.shape
    nj = S // ts
    hb = ts // HALO
    const = lambda b, j: (0, 0)
    return pl.pallas_call(
        _conv_body,
        grid=(B, nj),
        in_specs=[
            pl.BlockSpec((1, ts, C), lambda b, j: (b, j, 0)),
            pl.BlockSpec((1, HALO, C), lambda b, j: (b, jnp.maximum(j * hb - 1, 0), 0)),
            pl.BlockSpec((1, HALO, C), lambda b, j: (b, 0, 0)),
            pl.BlockSpec((HALO, C), const),
            pl.BlockSpec((1, C), const),
            pl.BlockSpec((1, C), const),
            pl.BlockSpec((1, C), const),
        ],
        out_specs=pl.BlockSpec((1, ts, C), lambda b, j: (b, j, 0)),
        out_shape=jax.ShapeDtypeStruct((B, S, C), BF16),
        scratch_shapes=[pltpu.VMEM((ts + HALO, C), F32), pltpu.VMEM((ts, C), F32)],
        compiler_params=_cparams(("arbitrary", "arbitrary")),
        name="conv_module",
    )(glu3, glu3, halo0, conv_w32, conv_b, cln_g, cln_b)


def _pair_attend(qp, kb, vb, bias, first_valid):
    R = qp.shape[0]
    lane = lax.broadcasted_iota(jnp.int32, (R, LANES), 1)
    zero = jnp.zeros_like(qp)
    qq = jnp.concatenate([jnp.where(lane < HEAD_DIM, qp, zero),
                          jnp.where(lane >= HEAD_DIM, qp, zero)], axis=0)
    s = lax.dot_general(qq, kb, (((1,), (1,)), ((), ())), preferred_element_type=F32)
    s = s + bias
    if first_valid is not None:
        kidx = lax.broadcasted_iota(jnp.int32, s.shape, 1)
        s = jnp.where(kidx >= first_valid, s, NEG_INF)
    m = jnp.max(s, axis=-1, keepdims=True)
    e = jnp.exp(s - m)
    l = jnp.sum(e, axis=-1, keepdims=True)
    o2 = jnp.dot(e.astype(BF16), vb, preferred_element_type=F32) * (1.0 / l)
    return jnp.where(lane < HEAD_DIM, o2[:R], o2[R:])


def _attn_prompt_body(q_ref, kp_ref, kc_ref, vp_ref, vc_ref, bias_ref, o_ref, kcat, vcat):
    j = pl.program_id(1)
    tq = q_ref.shape[0]
    nch = tq // CHUNK
    kcat[0:CHUNK, :] = jnp.zeros((CHUNK, ATTN_WIDTH), BF16)
    vcat[0:CHUNK, :] = jnp.zeros((CHUNK, ATTN_WIDTH), BF16)
    kcat[CHUNK:CHUNK + tq, :] = kp_ref[...]
    vcat[CHUNK:CHUNK + tq, :] = vp_ref[...]
    kcat[CHUNK + tq:, :] = kc_ref[...]
    vcat[CHUNK + tq:, :] = vc_ref[...]
    hist = jnp.where(j == 0, 0, tq)

    def chunk(c, carry):
        r0 = pl.multiple_of(c * CHUNK, CHUNK)
        first_valid = CHUNK + tq - hist - c * CHUNK
        for p in range(N_PAIRS):
            lanes = slice(p * LANES, (p + 1) * LANES)
            o = _pair_attend(q_ref[pl.ds(r0, CHUNK), lanes],
                             kcat[pl.ds(r0, BAND), lanes],
                             vcat[pl.ds(r0, BAND), lanes],
                             bias_ref[p], first_valid)
            o_ref[pl.ds(r0, CHUNK), lanes] = o.astype(BF16)
        return carry

    lax.fori_loop(0, nch, chunk, 0)


def _attn_prompt(q, k, v, bias, *, B, S, tq):
    assert tq == WINDOW - CHUNK, "key band of a query tile = previous tile + own tile"
    nj = S // tq
    cur = lambda b, j: (b * nj + j, 0)
    prev = lambda b, j: (b * nj + jnp.maximum(j - 1, 0), 0)
    blk = (tq, ATTN_WIDTH)
    return pl.pallas_call(
        _attn_prompt_body,
        grid=(B, nj),
        in_specs=[
            pl.BlockSpec(blk, cur),
            pl.BlockSpec(blk, prev),
            pl.BlockSpec(blk, cur),
            pl.BlockSpec(blk, prev),
            pl.BlockSpec(blk, cur),
            pl.BlockSpec((N_PAIRS, 2 * CHUNK, BAND), lambda b, j: (0, 0, 0)),
        ],
        out_specs=pl.BlockSpec(blk, cur),
        out_shape=jax.ShapeDtypeStruct((B * S, ATTN_WIDTH), BF16),
        scratch_shapes=[pltpu.VMEM((CHUNK + 2 * tq, ATTN_WIDTH), BF16),
                        pltpu.VMEM((CHUNK + 2 * tq, ATTN_WIDTH), BF16)],
        compiler_params=_cparams(("arbitrary", "arbitrary")),
        name="attn_prompt",
    )(q, k, k, v, v, bias)


def _attn_sample_body(q_ref, k_ref, v_ref, bias_ref, o_ref):
    for p in range(N_PAIRS):
        lanes = slice(p * LANES, (p + 1) * LANES)
        o = _pair_attend(q_ref[0, :, lanes], k_ref[0, :, lanes], v_ref[0, :, lanes], bias_ref[p], None)
        o_ref[0, :, lanes] = o.astype(BF16)


def _attn_sample(q3, k3, v3, bias):
    B, R, _ = q3.shape
    return pl.pallas_call(
        _attn_sample_body,
        grid=(B,),
        in_specs=[
            pl.BlockSpec((1, R, ATTN_WIDTH), lambda b: (b, 0, 0)),
            pl.BlockSpec((1, BAND, ATTN_WIDTH), lambda b: (b, 0, 0)),
            pl.BlockSpec((1, BAND, ATTN_WIDTH), lambda b: (b, 0, 0)),
            pl.BlockSpec((N_PAIRS, 2 * R, BAND), lambda b: (0, 0, 0)),
        ],
        out_specs=pl.BlockSpec((1, R, ATTN_WIDTH), lambda b: (b, 0, 0)),
        out_shape=jax.ShapeDtypeStruct((B, R, ATTN_WIDTH), BF16),
        compiler_params=_cparams(("arbitrary",)),
        name="attn_sample",
    )(q3, k3, v3, bias)


def _split_bf16(x):
    hi = x.astype(BF16)
    lo = (x - hi.astype(F32)).astype(BF16)
    return hi, lo


def _out_proj_body(ap_ref, as_ref, cp_ref, cs_ref, xp_ref, xs_ref, wa_ref, wc_ref, gffn_ref,
                   wrh_ref, wrl_ref, br_ref, xm_ref, route_ref, *, ntp):
    is_p = pl.program_id(0) < ntp
    a = jnp.where(is_p, ap_ref[...], as_ref[...])
    c = jnp.where(is_p, cp_ref[...], cs_ref[...])
    x = jnp.where(is_p, xp_ref[...], xs_ref[...])
    mix = jnp.dot(a, wa_ref[...], preferred_element_type=F32)
    mix = mix + jnp.dot(c, wc_ref[...], preferred_element_type=F32)
    xm = x + mix
    xm_ref[...] = xm
    h = _rms(xm, gffn_ref[...])
    hh, hl = _split_bf16(h)
    wh = wrh_ref[...]
    lg = (jnp.dot(hh, wh, preferred_element_type=F32)
          + jnp.dot(hl, wh, preferred_element_type=F32)
          + jnp.dot(hh, wrl_ref[...], preferred_element_type=F32)) + br_ref[...]
    lane = lax.broadcasted_iota(jnp.int32, lg.shape, 1)
    big = jnp.int32(1 << 20)

    def top1(mask):
        v = jnp.max(jnp.where(mask, lg, -jnp.inf), axis=-1, keepdims=True)
        i = jnp.min(jnp.where(mask & (lg == v), lane, big), axis=-1, keepdims=True)
        return v, i

    gmask = lane < N_GROUPS
    gv, gi = top1(gmask)
    p_sel = 1.0 / jnp.sum(jnp.where(gmask, jnp.exp(lg - gv), 0.0), axis=-1, keepdims=True)
    lo_lane = N_GROUPS + gi * EXPERTS_PER_GROUP
    emask = (lane >= lo_lane) & (lane < lo_lane + EXPERTS_PER_GROUP)
    v1, i1 = top1(emask)
    v2, i2 = top1(emask & (lane != i1))
    e2 = jnp.exp(v2 - v1)
    g1 = p_sel / (1.0 + e2)
    g2 = p_sel * e2 / (1.0 + e2)
    route = jnp.where(lane == 0, (i1 - N_GROUPS).astype(F32),
                      jnp.where(lane == 1, (i2 - N_GROUPS).astype(F32),
                                jnp.where(lane == 2, g1, jnp.where(lane == 3, g2, 0.0))))
    route_ref[...] = route


def _out_proj(attn_p, attn_s, conv_p, conv_s, x_p, x_s, w_oa, w_oc, gffn, wr_hi, wr_lo, b_r, *, tm):
    Tp, Ts = x_p.shape[0], x_s.shape[0]
    ntp, nts = Tp // tm, Ts // tm
    pmap = lambda i: (jnp.minimum(i, ntp - 1), 0)
    smap = lambda i: (jnp.maximum(i - ntp, 0), 0)
    row = lambda i: (i, 0)
    const = lambda i: (0, 0)
    return pl.pallas_call(
        functools.partial(_out_proj_body, ntp=ntp),
        grid=(ntp + nts,),
        in_specs=[
            pl.BlockSpec((tm, ATTN_WIDTH), pmap),
            pl.BlockSpec((tm, ATTN_WIDTH), smap),
            pl.BlockSpec((tm, CONV_CH), pmap),
            pl.BlockSpec((tm, CONV_CH), smap),
            pl.BlockSpec((tm, D_MODEL), pmap),
            pl.BlockSpec((tm, D_MODEL), smap),
            pl.BlockSpec((ATTN_WIDTH, D_MODEL), const),
            pl.BlockSpec((CONV_CH, D_MODEL), const),
            pl.BlockSpec((1, D_MODEL), const),
            pl.BlockSpec((D_MODEL, ROUTE_W), const),
            pl.BlockSpec((D_MODEL, ROUTE_W), const),
            pl.BlockSpec((1, ROUTE_W), const),
        ],
        out_specs=(pl.BlockSpec((tm, D_MODEL), row), pl.BlockSpec((tm, ROUTE_W), row)),
        out_shape=(jax.ShapeDtypeStruct((Tp + Ts, D_MODEL), F32),
                   jax.ShapeDtypeStruct((Tp + Ts, ROUTE_W), F32)),
        compiler_params=_cparams(("arbitrary",)),
        name="out_proj",
    )(attn_p, attn_s, conv_p, conv_s, x_p, x_s, w_oa, w_oc, gffn, wr_hi, wr_lo, b_r)


def _moe_body(be_ref, nu_ref, tok_ref, xm_hbm, gffn_ref, wg_ref, wu_ref, wd_ref, out_ref, xbuf, sem):
    b = pl.program_id(0)
    tm = out_ref.shape[0]
    nused = nu_ref[0]

    def issue(blk, slot):
        def body(r, carry):
            tok = tok_ref[blk * tm + r]
            pltpu.make_async_copy(xm_hbm.at[pl.ds(tok, 1)], xbuf.at[slot, pl.ds(r, 1)], sem.at[slot]).start()
            return carry
        lax.fori_loop(0, tm, body, 0, unroll=DMA_UNROLL)

    @pl.when(b == 0)
    def _():
        issue(0, 0)

    @pl.when(b < nused)
    def _():
        slot = b % 2

        @pl.when(b + 1 < nused)
        def _():
            issue(b + 1, 1 - slot)

        pltpu.make_async_copy(xm_hbm.at[pl.ds(0, tm)], xbuf.at[slot], sem.at[slot]).wait()
        h = _rms(xbuf[slot], gffn_ref[...]).astype(BF16)
        g = jnp.dot(h, wg_ref[0], preferred_element_type=F32)
        u = jnp.dot(h, wu_ref[0], preferred_element_type=F32)
        a = (g * jax.nn.sigmoid(g) * u).astype(BF16)
        out_ref[...] = jnp.dot(a, wd_ref[0], preferred_element_type=F32)

    @pl.when(b >= nused)
    def _():
        out_ref[...] = jnp.zeros(out_ref.shape, F32)


def _moe(block_e, nused, slot_tok, xm_all, gffn, wg, wu, wd, *, tm, nb):
    wmap = lambda b, be, nu, tok: (be[b], 0, 0)
    grid_spec = pltpu.PrefetchScalarGridSpec(
        num_scalar_prefetch=3,
        grid=(nb,),
        in_specs=[
            pl.BlockSpec(memory_space=pl.ANY),
            pl.BlockSpec((1, D_MODEL), lambda b, be, nu, tok: (0, 0)),
            pl.BlockSpec((1, D_MODEL, D_EXPERT), wmap),
            pl.BlockSpec((1, D_MODEL, D_EXPERT), wmap),
            pl.BlockSpec((1, D_EXPERT, D_MODEL), wmap),
        ],
        out_specs=pl.BlockSpec((tm, D_MODEL), lambda b, be, nu, tok: (b, 0)),
        scratch_shapes=[pltpu.VMEM((2, tm, D_MODEL), F32), pltpu.SemaphoreType.DMA((2,))],
    )
    return pl.pallas_call(
        _moe_body,
        grid_spec=grid_spec,
        out_shape=jax.ShapeDtypeStruct((nb * tm, D_MODEL), F32),
        compiler_params=_cparams(("arbitrary",), row_dma=True),
        name="moe",
    )(block_e, nused, slot_tok, xm_all, gffn, wg, wu, wd)


def _final_body(dest_ref, xm_ref, route_ref, p_ref, mo_hbm, gple_ref, wpg_ref, bpg_ref, wpe_ref,
                y_ref, ybuf, sem, *, tok0):
    i = pl.program_id(0)
    nt = pl.num_programs(0)
    tm = y_ref.shape[0]

    def issue(tile, slot):
        def body(r, carry):
            base = 2 * (tok0 + tile * tm + r)
            for kk in range(2):
                d = dest_ref[base + kk]
                pltpu.make_async_copy(mo_hbm.at[pl.ds(d, 1)], ybuf.at[slot, kk, pl.ds(r, 1)], sem.at[slot]).start()
            return carry
        lax.fori_loop(0, tm, body, 0, unroll=DMA_UNROLL)

    @pl.when(i == 0)
    def _():
        issue(0, 0)

    slot = i % 2

    @pl.when(i + 1 < nt)
    def _():
        issue(i + 1, 1 - slot)

    for kk in range(2):
        pltpu.make_async_copy(mo_hbm.at[pl.ds(0, tm)], ybuf.at[slot, kk], sem.at[slot]).wait()
    route = route_ref[...]
    x2 = xm_ref[...] + route[:, 2:3] * ybuf[slot, 0] + route[:, 3:4] * ybuf[slot, 1]
    r = _rms(x2, gple_ref[...]).astype(BF16)
    z = jnp.dot(r, wpg_ref[...], preferred_element_type=F32) + bpg_ref[...]
    pe = jnp.dot(p_ref[...].astype(BF16), wpe_ref[...], preferred_element_type=F32)
    y_ref[...] = x2 + jax.nn.sigmoid(z) * pe


def _final(dest, xm_all, route, p2d, moe_out, gple, wpg, bpg, wpe, *, tm, blk0):
    T = p2d.shape[0]
    nt = T // tm
    const = lambda i, d: (0, 0)
    grid_spec = pltpu.PrefetchScalarGridSpec(
        num_scalar_prefetch=1,
        grid=(nt,),
        in_specs=[
            pl.BlockSpec((tm, D_MODEL), lambda i, d: (i + blk0, 0)),
            pl.BlockSpec((tm, ROUTE_W), lambda i, d: (i + blk0, 0)),
            pl.BlockSpec((tm, PLE_DIM), lambda i, d: (i, 0)),
            pl.BlockSpec(memory_space=pl.ANY),
            pl.BlockSpec((1, D_MODEL), const),
            pl.BlockSpec((D_MODEL, D_MODEL), const),
            pl.BlockSpec((1, D_MODEL), const),
            pl.BlockSpec((PLE_DIM, D_MODEL), const),
        ],
        out_specs=pl.BlockSpec((tm, D_MODEL), lambda i, d: (i, 0)),
        scratch_shapes=[pltpu.VMEM((2, 2, tm, D_MODEL), F32), pltpu.SemaphoreType.DMA((2,))],
    )
    return pl.pallas_call(
        functools.partial(_final_body, tok0=blk0 * tm),
        grid_spec=grid_spec,
        out_shape=jax.ShapeDtypeStruct((T, D_MODEL), F32),
        compiler_params=_cparams(("arbitrary",), row_dma=True),
        name="final",
    )(dest, xm_all, route, p2d, moe_out, gple, wpg, bpg, wpe)


def _pair_stack(bias):
    H, R, K = bias.shape
    return bias.reshape(H // 2, 2 * R, K)


def _toeplitz_bias(rel_bias, R, c):
    u = np.arange(c - (BAND - 1), c + R)[::-1]
    strip = rel_bias[:, np.clip(u, -REL_CLIP, REL_CLIP) + REL_CLIP]
    return jnp.stack([strip[:, R - 1 - i:R - 1 - i + BAND] for i in range(R)], axis=1)


def _prompt_bias(rel_bias):
    tbl = _toeplitz_bias(rel_bias, CHUNK, BAND - CHUNK)
    tbl = jnp.where((np.arange(BAND) >= CHUNK)[None, None, :], tbl, NEG_INF)
    return _pair_stack(tbl.astype(F32))


def _sample_bias(rel_bias, S, L_all):
    q_pos = PAST_LEN + np.arange(S)
    k_pos = PAST_LEN + S - L_all + np.arange(BAND)
    dc = q_pos[:, None] // CHUNK - k_pos[None, :] // CHUNK
    mask = (dc >= 0) & (dc <= N_PREV_CHUNKS) & (np.arange(BAND) < L_all)[None, :]
    tbl = jnp.where(mask[None], _toeplitz_bias(rel_bias, S, L_all - S), NEG_INF)
    return _pair_stack(tbl.astype(F32))


def _routing_tables(route_all, *, tm, nb):
    T_all = route_all.shape[0]
    A = 2 * T_all
    e_flat = route_all[:, 0:2].astype(jnp.int32).reshape(A)
    onehot = e_flat[:, None] == jnp.arange(N_EXPERTS, dtype=jnp.int32)[None, :]
    ch = LANES
    assert A % ch == 0 and A < (1 << 24)
    oh3 = onehot.astype(F32).reshape(A // ch, ch, N_EXPERTS)
    tri = jnp.asarray(np.tril(np.ones((ch, ch), np.float32)))
    within = jnp.einsum('ij,cjk->cik', tri, oh3)
    tot = within[:, -1, :]
    base = jnp.cumsum(tot, axis=0) - tot
    counts = (base[-1] + tot[-1]).astype(jnp.int32)
    pcounts = (counts + tm - 1) // tm * tm
    pends = jnp.cumsum(pcounts)
    pstarts = pends - pcounts
    slot = within + base[:, None, :] - 1.0 + pstarts.astype(F32)[None, None, :]
    dest = jnp.sum(jnp.where(oh3 > 0, slot, 0.0), axis=-1).astype(jnp.int32).reshape(A)
    nused = (pends[-1] // tm).astype(jnp.int32)
    blk = jnp.minimum(jnp.arange(nb, dtype=jnp.int32), nused - 1)
    block_e = jnp.minimum(jnp.sum((pends[None, :] <= (blk * tm)[:, None]).astype(jnp.int32), axis=1),
                          N_EXPERTS - 1).astype(jnp.int32)
    slot_tok = jnp.zeros((nb * tm,), jnp.int32).at[dest].set(
        jnp.arange(A, dtype=jnp.int32) // 2, unique_indices=True)
    return block_e, nused.reshape(1), slot_tok, dest


def kernel(x_prompt, x_sample, p_prompt, p_sample, cache_k, cache_v, cache_conv, g_mix, w_in, qn_g, kn_g, rel_bias, conv_w, conv_b, cln_g, cln_b, w_out, g_ffn, w_grp, b_grp, w_rt, b_rt, w_e_gate, w_e_up, w_e_down, g_ple, w_pg, b_pg, w_pe):
    depth = g_mix.shape[0]
    assert depth == 1, "single-layer step"
    B, S, D = x_prompt.shape
    Bs, Ss, _ = x_sample.shape
    Tp, Ts = B * S, Bs * Ss
    T_all = Tp + Ts
    keep_p = min(WINDOW, S)
    L_all = cache_k.shape[2] + Ss
    keep_s = min(WINDOW, L_all)
    assert S >= CONV_K - 1 and Ss >= CONV_K - 1 and L_all <= BAND

    TM = 256
    TQ = WINDOW - CHUNK
    TM_E = 512
    assert S % TQ == 0 and S % TM == 0 and Ts % TM == 0 and Tp % TM == 0

    i = 0
    w_in_b = w_in[i].astype(BF16)
    w_oa = w_out[i, :ATTN_WIDTH].astype(BF16)
    w_oc = w_out[i, ATTN_WIDTH:].astype(BF16)
    w_pg_b = w_pg[i].astype(BF16)
    w_pe_b = w_pe[i].astype(BF16)
    wg_b = w_e_gate[i].astype(BF16)
    wu_b = w_e_up[i].astype(BF16)
    wd_b = w_e_down[i].astype(BF16)
    row = lambda a: a.reshape(1, -1).astype(F32)
    gmix, gffn, gple, bpg = row(g_mix[i]), row(g_ffn[i]), row(g_ple[i]), row(b_pg[i])
    qg = row(jnp.tile(qn_g[i], N_HEADS))
    kg = row(jnp.tile(kn_g[i], N_HEADS))
    sgn = jnp.asarray(np.where(np.arange(LANES) < HEAD_DIM, 1.0, -1.0).reshape(1, LANES), F32)
    conv_w32 = jnp.pad(conv_w[i], ((0, HALO - CONV_K), (0, 0)))
    convb, clng, clnb = row(conv_b[i]), row(cln_g[i]), row(cln_b[i])
    w_r = jnp.pad(jnp.concatenate([w_grp[i], w_rt[i]], axis=1), ((0, 0), (0, ROUTE_W - N_GROUPS - N_EXPERTS)))
    wr_hi = w_r.astype(BF16)
    wr_lo = (w_r - wr_hi.astype(F32)).astype(BF16)
    b_r = jnp.pad(jnp.concatenate([b_grp[i], b_rt[i]]), (0, ROUTE_W - N_GROUPS - N_EXPERTS)).reshape(1, ROUTE_W)
    bias_p = _prompt_bias(rel_bias[i])
    bias_s = _sample_bias(rel_bias[i], Ss, L_all)

    nj = S // TM
    ntail = -(-keep_p // TM)
    tail_map = lambda t: ((t // nj) * ntail + jnp.maximum(t % nj - (nj - ntail), 0), 0)
    q_p, k_p, v_p, glu_p, kf_p, vf_p, gf_p = _in_proj(
        x_prompt.reshape(Tp, D), gmix, w_in_b, qg, kg, sgn, tm=TM,
        tail_rows=B * ntail * TM, kf_map=tail_map,
        gf_rows=HALO, gf_blocks=B, gf_map=lambda t: (t // nj, 0))
    cut = ntail * TM - keep_p
    new_k_p = kf_p.reshape(B, ntail * TM, N_HEADS, HEAD_DIM)[:, cut:][None]
    new_v_p = vf_p.reshape(B, ntail * TM, N_HEADS, HEAD_DIM)[:, cut:][None]
    new_c_p = gf_p.reshape(B, HALO, CONV_CH)[:, HALO - (CONV_K - 1):][None]

    conv_p = _conv_module(glu_p.reshape(B, S, CONV_CH), jnp.zeros((B, HALO, CONV_CH), F32),
                          conv_w32, convb, clng, clnb, ts=TM)
    attn_p = _attn_prompt(q_p, k_p, v_p, bias_p, B=B, S=S, tq=TQ)

    ident = lambda t: (t, 0)
    q_s, k_s, v_s, glu_s, kf_s, vf_s, gf_s = _in_proj(
        x_sample.reshape(Ts, D), gmix, w_in_b, qg, kg, sgn, tm=TM,
        tail_rows=Ts, kf_map=ident, gf_rows=TM, gf_blocks=Ts // TM, gf_map=ident)
    ck = cache_k[i].reshape(Bs, -1, ATTN_WIDTH)
    cv = cache_v[i].reshape(Bs, -1, ATTN_WIDTH)
    k_all = jnp.concatenate([ck, kf_s.reshape(Bs, Ss, ATTN_WIDTH)], axis=1)
    v_all = jnp.concatenate([cv, vf_s.reshape(Bs, Ss, ATTN_WIDTH)], axis=1)
    new_k_s = k_all[:, L_all - keep_s:].reshape(Bs, keep_s, N_HEADS, HEAD_DIM)[None]
    new_v_s = v_all[:, L_all - keep_s:].reshape(Bs, keep_s, N_HEADS, HEAD_DIM)[None]
    conv_in_s = jnp.concatenate([cache_conv[i], gf_s.reshape(Bs, Ss, CONV_CH)], axis=1)
    new_c_s = conv_in_s[:, -(CONV_K - 1):][None]
    padk = ((0, 0), (0, BAND - L_all), (0, 0))
    attn_s = _attn_sample(q_s.reshape(Bs, Ss, ATTN_WIDTH), jnp.pad(k_all.astype(BF16), padk),
                          jnp.pad(v_all.astype(BF16), padk), bias_s)
    halo_s = jnp.pad(cache_conv[i], ((0, 0), (HALO - (CONV_K - 1), 0), (0, 0)))
    conv_s = _conv_module(glu_s.reshape(Bs, Ss, CONV_CH), halo_s, conv_w32, convb, clng, clnb, ts=Ss)

    xm_all, route_all = _out_proj(attn_p, attn_s.reshape(Ts, ATTN_WIDTH),
                                  conv_p.reshape(Tp, CONV_CH), conv_s.reshape(Ts, CONV_CH),
                                  x_prompt.reshape(Tp, D), x_sample.reshape(Ts, D),
                                  w_oa, w_oc, gffn, wr_hi, wr_lo, b_r, tm=TM)

    nb = -(-2 * T_all // TM_E) + N_EXPERTS
    block_e, nused, slot_tok, dest = _routing_tables(route_all, tm=TM_E, nb=nb)
    moe_out = _moe(block_e, nused, slot_tok, xm_all, gffn, wg_b, wu_b, wd_b, tm=TM_E, nb=nb)

    y_p = _final(dest, xm_all, route_all, p_prompt[i].reshape(Tp, PLE_DIM), moe_out,
                 gple, w_pg_b, bpg, w_pe_b, tm=TM, blk0=0)
    y_s = _final(dest, xm_all, route_all, p_sample[i].reshape(Ts, PLE_DIM), moe_out,
                 gple, w_pg_b, bpg, w_pe_b, tm=TM, blk0=Tp // TM)
    return (y_p.reshape(B, S, D), y_s.reshape(Bs, Ss, D), new_k_p, new_v_p, new_c_p,
            new_k_s, new_v_s, new_c_s)
```

```python
import functools

import numpy as np
import jax
import jax.numpy as jnp
from jax import lax
from jax.experimental import pallas as pl
from jax.experimental.pallas import tpu as pltpu

F32 = jnp.float32
BF16 = jnp.bfloat16

D_MODEL = 2048
ATTN_WIDTH = 1024
CONV_CH = 1024
HEAD_DIM = 64
N_HEADS = 16
N_PAIRS = N_HEADS // 2
IN_WIDTH = 3 * ATTN_WIDTH + 2 * CONV_CH
CHUNK = 64
N_PREV_CHUNKS = 8
WINDOW = (N_PREV_CHUNKS + 1) * CHUNK
BAND = WINDOW + CHUNK
CONV_K = 31
HALO = 32
REL_CLIP = 128
PAST_LEN = 4096
N_GROUPS = 4
EXPERTS_PER_GROUP = 8
N_EXPERTS = 32
D_EXPERT = 1024
PLE_DIM = 256
EPS = 1e-6
NEG_INF = -1e30
LANES = 128
SUBLANES = 8
ROUTE_W = 128

VMEM_LIMIT = 56 * 1024 * 1024


DMA_UNROLL = 8


def _cparams(sem, row_dma=False):
    return pltpu.CompilerParams(dimension_semantics=sem, vmem_limit_bytes=VMEM_LIMIT,
                                disable_bounds_checks=row_dma)


def _rms(x, g):
    return x * lax.rsqrt(jnp.mean(x * x, axis=-1, keepdims=True) + EPS) * g


def _in_proj_body(x_ref, gmix_ref, w_ref, qg_ref, kg_ref, sgn_ref,
                  q_ref, k_ref, v_ref, glu_ref, kf_ref, vf_ref, gf_ref):
    h = _rms(x_ref[...], gmix_ref[...]).astype(BF16)
    sgn = sgn_ref[...]

    def seg(i):
        return jnp.dot(h, w_ref[:, i * ATTN_WIDTH:(i + 1) * ATTN_WIDTH], preferred_element_type=F32)

    def head_norm(u, g_ref, p):
        up = u[:, p * LANES:(p + 1) * LANES]
        sq = up * up
        s = jnp.sum(sq, axis=-1, keepdims=True)
        d = jnp.sum(sq * sgn, axis=-1, keepdims=True)
        ms = (s + sgn * d) * (0.5 / HEAD_DIM)
        return up * lax.rsqrt(ms + EPS) * g_ref[:, p * LANES:(p + 1) * LANES]

    uq = seg(0)
    for p in range(N_PAIRS):
        q_ref[:, p * LANES:(p + 1) * LANES] = (head_norm(uq, qg_ref, p) * (HEAD_DIM ** -0.5)).astype(BF16)
    uk = seg(1)
    for p in range(N_PAIRS):
        kn = head_norm(uk, kg_ref, p)
        kf_ref[:, p * LANES:(p + 1) * LANES] = kn
        k_ref[:, p * LANES:(p + 1) * LANES] = kn.astype(BF16)
    uv = seg(2)
    vf_ref[...] = uv
    v_ref[...] = uv.astype(BF16)
    glu = seg(3) * jax.nn.sigmoid(seg(4))
    glu_ref[...] = glu.astype(BF16)
    gr = gf_ref.shape[0]
    gf_ref[...] = glu[glu.shape[0] - gr:, :]


def _in_proj(x2d, gmix, w_in, qg, kg, sgn, *, tm, tail_rows, kf_map, gf_rows, gf_blocks, gf_map):
    T = x2d.shape[0]
    nt = T // tm
    row = lambda i: (i, 0)
    const = lambda i: (0, 0)
    out_shape = (
        jax.ShapeDtypeStruct((T, ATTN_WIDTH), BF16),
        jax.ShapeDtypeStruct((T, ATTN_WIDTH), BF16),
        jax.ShapeDtypeStruct((T, ATTN_WIDTH), BF16),
        jax.ShapeDtypeStruct((T, CONV_CH), BF16),
        jax.ShapeDtypeStruct((tail_rows, ATTN_WIDTH), F32),
        jax.ShapeDtypeStruct((tail_rows, ATTN_WIDTH), F32),
        jax.ShapeDtypeStruct((gf_blocks * gf_rows, CONV_CH), F32),
    )
    return pl.pallas_call(
        _in_proj_body,
        grid=(nt,),
        in_specs=[
            pl.BlockSpec((tm, D_MODEL), row),
            pl.BlockSpec((1, D_MODEL), const),
            pl.BlockSpec((D_MODEL, IN_WIDTH), const, pipeline_mode=pl.Buffered(1)),
            pl.BlockSpec((1, ATTN_WIDTH), const),
            pl.BlockSpec((1, ATTN_WIDTH), const),
            pl.BlockSpec((1, LANES), const),
        ],
        out_specs=(
            pl.BlockSpec((tm, ATTN_WIDTH), row),
            pl.BlockSpec((tm, ATTN_WIDTH), row),
            pl.BlockSpec((tm, ATTN_WIDTH), row),
            pl.BlockSpec((tm, CONV_CH), row),
            pl.BlockSpec((tm, ATTN_WIDTH), kf_map),
            pl.BlockSpec((tm, ATTN_WIDTH), kf_map),
            pl.BlockSpec((gf_rows, CONV_CH), gf_map),
        ),
        out_shape=out_shape,
        compiler_params=_cparams(("arbitrary",)),
        name="in_proj",
    )(x2d, gmix, w_in, qg, kg, sgn)


CONV_ROWS = 64


def _conv_body(g_ref, gh_ref, h0_ref, w_ref, b_ref, lg_ref, lb_ref, o_ref, xc_ref, y_ref, xs_ref):
    j = pl.program_id(1)
    ts = g_ref.shape[1]
    first = j == 0
    xc_ref[0:HALO, :] = jnp.where(first, h0_ref[0], gh_ref[0].astype(F32))
    xc_ref[HALO:, :] = g_ref[0].astype(F32)
    off = HALO - (CONV_K - 1)
    rc = min(CONV_ROWS, ts)
    ns = ts + HALO - SUBLANES
    for rho in range(1, SUBLANES):
        xs_ref[rho - 1, :, :] = xc_ref[rho:rho + ns, :]
    for p in range(CONV_CH // LANES):
        lanes = slice(p * LANES, (p + 1) * LANES)
        wp = w_ref[:, lanes]
        bp = b_ref[:, lanes]
        for r in range(ts // rc):
            r0 = r * rc
            acc = jnp.broadcast_to(bp, (rc, LANES))
            for t in range(CONV_K):
                rho = (off + t) % SUBLANES
                a0 = r0 + (off + t) - rho
                win = xc_ref[a0:a0 + rc, lanes] if rho == 0 else xs_ref[rho - 1, a0:a0 + rc, lanes]
                acc = acc + wp[t:t + 1, :] * win
            y_ref[r0:r0 + rc, lanes] = acc
    y = y_ref[...]
    mu = jnp.mean(y, axis=-1, keepdims=True)
    yc = y - mu
    z = yc * lax.rsqrt(jnp.mean(yc * yc, axis=-1, keepdims=True) + EPS) * lg_ref[...] + lb_ref[...]
    o_ref[0] = (z * jax.nn.sigmoid(z)).astype(BF16)


def _conv_module(glu3, halo0, conv_w32, conv_b, cln_g, cln_b, *, ts):
    B, S, C = glu3.shape
    nj = S // ts
    hb = ts // HALO
    const = lambda b, j: (0, 0)
    return pl.pallas_call(
        _conv_body,
        grid=(B, nj),
        in_specs=[
            pl.BlockSpec((1, ts, C), lambda b, j: (b, j, 0)),
            pl.BlockSpec((1, HALO, C), lambda b, j: (b, jnp.maximum(j * hb - 1, 0), 0)),
            pl.BlockSpec((1, HALO, C), lambda b, j: (b, 0, 0)),
            pl.BlockSpec((HALO, C), const),
            pl.BlockSpec((1, C), const),
            pl.BlockSpec((1, C), const),
            pl.BlockSpec((1, C), const),
        ],
        out_specs=pl.BlockSpec((1, ts, C), lambda b, j: (b, j, 0)),
        out_shape=jax.ShapeDtypeStruct((B, S, C), BF16),
        scratch_shapes=[pltpu.VMEM((ts + HALO, C), F32), pltpu.VMEM((ts, C), F32),
                        pltpu.VMEM((SUBLANES - 1, ts + HALO - SUBLANES, C), F32)],
        compiler_params=_cparams(("arbitrary", "arbitrary")),
        name="conv_module",
    )(glu3, glu3, halo0, conv_w32, conv_b, cln_g, cln_b)


def _pair_attend(qp, kb, vb, bias, first_valid):
    R = qp.shape[0]
    lane = lax.broadcasted_iota(jnp.int32, (R, LANES), 1)
    zero = jnp.zeros_like(qp)
    qq = jnp.concatenate([jnp.where(lane < HEAD_DIM, qp, zero),
                          jnp.where(lane >= HEAD_DIM, qp, zero)], axis=0)
    s = lax.dot_general(qq, kb, (((1,), (1,)), ((), ())), preferred_element_type=F32)
    s = s + bias
    if first_valid is not None:
        kidx = lax.broadcasted_iota(jnp.int32, s.shape, 1)
        s = jnp.where(kidx >= first_valid, s, NEG_INF)
    m = jnp.max(s, axis=-1, keepdims=True)
    e = jnp.exp(s - m)
    l = jnp.sum(e, axis=-1, keepdims=True)
    o2 = jnp.dot(e.astype(BF16), vb, preferred_element_type=F32) * (1.0 / l)
    return jnp.where(lane < HEAD_DIM, o2[:R], o2[R:])


def _attn_prompt_body(q_ref, kp_ref, kc_ref, vp_ref, vc_ref, bias_ref, o_ref, kcat, vcat):
    j = pl.program_id(1)
    tq = q_ref.shape[0]
    nch = tq // CHUNK
    kcat[0:CHUNK, :] = jnp.zeros((CHUNK, ATTN_WIDTH), BF16)
    vcat[0:CHUNK, :] = jnp.zeros((CHUNK, ATTN_WIDTH), BF16)
    kcat[CHUNK:CHUNK + tq, :] = kp_ref[...]
    vcat[CHUNK:CHUNK + tq, :] = vp_ref[...]
    kcat[CHUNK + tq:, :] = kc_ref[...]
    vcat[CHUNK + tq:, :] = vc_ref[...]
    hist = jnp.where(j == 0, 0, tq)

    def chunk(c, carry):
        r0 = pl.multiple_of(c * CHUNK, CHUNK)
        first_valid = CHUNK + tq - hist - c * CHUNK
        for p in range(N_PAIRS):
            lanes = slice(p * LANES, (p + 1) * LANES)
            o = _pair_attend(q_ref[pl.ds(r0, CHUNK), lanes],
                             kcat[pl.ds(r0, BAND), lanes],
                             vcat[pl.ds(r0, BAND), lanes],
                             bias_ref[p], first_valid)
            o_ref[pl.ds(r0, CHUNK), lanes] = o.astype(BF16)
        return carry

    lax.fori_loop(0, nch, chunk, 0)


def _attn_prompt(q, k, v, bias, *, B, S, tq):
    assert tq == WINDOW - CHUNK, "key band of a query tile = previous tile + own tile"
    nj = S // tq
    cur = lambda b, j: (b * nj + j, 0)
    prev = lambda b, j: (b * nj + jnp.maximum(j - 1, 0), 0)
    blk = (tq, ATTN_WIDTH)
    return pl.pallas_call(
        _attn_prompt_body,
        grid=(B, nj),
        in_specs=[
            pl.BlockSpec(blk, cur),
            pl.BlockSpec(blk, prev),
            pl.BlockSpec(blk, cur),
            pl.BlockSpec(blk, prev),
            pl.BlockSpec(blk, cur),
            pl.BlockSpec((N_PAIRS, 2 * CHUNK, BAND), lambda b, j: (0, 0, 0)),
        ],
        out_specs=pl.BlockSpec(blk, cur),
        out_shape=jax.ShapeDtypeStruct((B * S, ATTN_WIDTH), BF16),
        scratch_shapes=[pltpu.VMEM((CHUNK + 2 * tq, ATTN_WIDTH), BF16),
                        pltpu.VMEM((CHUNK + 2 * tq, ATTN_WIDTH), BF16)],
        compiler_params=_cparams(("arbitrary", "arbitrary")),
        name="attn_prompt",
    )(q, k, k, v, v, bias)


def _attn_sample_body(q_ref, k_ref, v_ref, bias_ref, o_ref):
    for p in range(N_PAIRS):
        lanes = slice(p * LANES, (p + 1) * LANES)
        o = _pair_attend(q_ref[0, :, lanes], k_ref[0, :, lanes], v_ref[0, :, lanes], bias_ref[p], None)
        o_ref[0, :, lanes] = o.astype(BF16)


def _attn_sample(q3, k3, v3, bias):
    B, R, _ = q3.shape
    return pl.pallas_call(
        _attn_sample_body,
        grid=(B,),
        in_specs=[
            pl.BlockSpec((1, R, ATTN_WIDTH), lambda b: (b, 0, 0)),
            pl.BlockSpec((1, BAND, ATTN_WIDTH), lambda b: (b, 0, 0)),
            pl.BlockSpec((1, BAND, ATTN_WIDTH), lambda b: (b, 0, 0)),
            pl.BlockSpec((N_PAIRS, 2 * R, BAND), lambda b: (0, 0, 0)),
        ],
        out_specs=pl.BlockSpec((1, R, ATTN_WIDTH), lambda b: (b, 0, 0)),
        out_shape=jax.ShapeDtypeStruct((B, R, ATTN_WIDTH), BF16),
        compiler_params=_cparams(("arbitrary",)),
        name="attn_sample",
    )(q3, k3, v3, bias)


def _split_bf16(x):
    hi = x.astype(BF16)
    lo = (x - hi.astype(F32)).astype(BF16)
    return hi, lo


def _out_proj_body(ap_ref, as_ref, cp_ref, cs_ref, xp_ref, xs_ref, wa_ref, wc_ref, gffn_ref,
                   wrh_ref, wrl_ref, br_ref, xm_ref, route_ref, *, ntp):
    is_p = pl.program_id(0) < ntp
    a = jnp.where(is_p, ap_ref[...], as_ref[...])
    c = jnp.where(is_p, cp_ref[...], cs_ref[...])
    x = jnp.where(is_p, xp_ref[...], xs_ref[...])
    mix = jnp.dot(a, wa_ref[...], preferred_element_type=F32)
    mix = mix + jnp.dot(c, wc_ref[...], preferred_element_type=F32)
    xm = x + mix
    xm_ref[...] = xm
    h = _rms(xm, gffn_ref[...])
    hh, hl = _split_bf16(h)
    wh = wrh_ref[...]
    lg = (jnp.dot(hh, wh, preferred_element_type=F32)
          + jnp.dot(hl, wh, preferred_element_type=F32)
          + jnp.dot(hh, wrl_ref[...], preferred_element_type=F32)) + br_ref[...]
    lane = lax.broadcasted_iota(jnp.int32, lg.shape, 1)
    big = jnp.int32(1 << 20)

    def top1(mask):
        v = jnp.max(jnp.where(mask, lg, -jnp.inf), axis=-1, keepdims=True)
        i = jnp.min(jnp.where(mask & (lg == v), lane, big), axis=-1, keepdims=True)
        return v, i

    gmask = lane < N_GROUPS
    gv, gi = top1(gmask)
    p_sel = 1.0 / jnp.sum(jnp.where(gmask, jnp.exp(lg - gv), 0.0), axis=-1, keepdims=True)
    lo_lane = N_GROUPS + gi * EXPERTS_PER_GROUP
    emask = (lane >= lo_lane) & (lane < lo_lane + EXPERTS_PER_GROUP)
    v1, i1 = top1(emask)
    v2, i2 = top1(emask & (lane != i1))
    e2 = jnp.exp(v2 - v1)
    g1 = p_sel / (1.0 + e2)
    g2 = p_sel * e2 / (1.0 + e2)
    route = jnp.where(lane == 0, (i1 - N_GROUPS).astype(F32),
                      jnp.where(lane == 1, (i2 - N_GROUPS).astype(F32),
                                jnp.where(lane == 2, g1, jnp.where(lane == 3, g2, 0.0))))
    route_ref[...] = route


def _out_proj(attn_p, attn_s, conv_p, conv_s, x_p, x_s, w_oa, w_oc, gffn, wr_hi, wr_lo, b_r, *, tm):
    Tp, Ts = x_p.shape[0], x_s.shape[0]
    ntp, nts = Tp // tm, Ts // tm
    pmap = lambda i: (jnp.minimum(i, ntp - 1), 0)
    smap = lambda i: (jnp.maximum(i - ntp, 0), 0)
    row = lambda i: (i, 0)
    const = lambda i: (0, 0)
    return pl.pallas_call(
        functools.partial(_out_proj_body, ntp=ntp),
        grid=(ntp + nts,),
        in_specs=[
            pl.BlockSpec((tm, ATTN_WIDTH), pmap),
            pl.BlockSpec((tm, ATTN_WIDTH), smap),
            pl.BlockSpec((tm, CONV_CH), pmap),
            pl.BlockSpec((tm, CONV_CH), smap),
            pl.BlockSpec((tm, D_MODEL), pmap),
            pl.BlockSpec((tm, D_MODEL), smap),
            pl.BlockSpec((ATTN_WIDTH, D_MODEL), const),
            pl.BlockSpec((CONV_CH, D_MODEL), const),
            pl.BlockSpec((1, D_MODEL), const),
            pl.BlockSpec((D_MODEL, ROUTE_W), const),
            pl.BlockSpec((D_MODEL, ROUTE_W), const),
            pl.BlockSpec((1, ROUTE_W), const),
        ],
        out_specs=(pl.BlockSpec((tm, D_MODEL), row), pl.BlockSpec((tm, ROUTE_W), row)),
        out_shape=(jax.ShapeDtypeStruct((Tp + Ts, D_MODEL), F32),
                   jax.ShapeDtypeStruct((Tp + Ts, ROUTE_W), F32)),
        compiler_params=_cparams(("arbitrary",)),
        name="out_proj",
    )(attn_p, attn_s, conv_p, conv_s, x_p, x_s, w_oa, w_oc, gffn, wr_hi, wr_lo, b_r)


def _moe_body(be_ref, nu_ref, tok_ref, xm_hbm, gffn_ref, wg_ref, wu_ref, wd_ref, out_ref, xbuf, sem):
    b = pl.program_id(0)
    tm = out_ref.shape[0]
    nused = nu_ref[0]

    def issue(blk, slot):
        def body(r, carry):
            tok = tok_ref[blk * tm + r]
            pltpu.make_async_copy(xm_hbm.at[pl.ds(tok, 1)], xbuf.at[slot, pl.ds(r, 1)], sem.at[slot]).start()
            return carry
        lax.fori_loop(0, tm, body, 0, unroll=DMA_UNROLL)

    @pl.when(b == 0)
    def _():
        issue(0, 0)

    @pl.when(b < nused)
    def _():
        slot = b % 2

        @pl.when(b + 1 < nused)
        def _():
            issue(b + 1, 1 - slot)

        pltpu.make_async_copy(xm_hbm.at[pl.ds(0, tm)], xbuf.at[slot], sem.at[slot]).wait()
        h = _rms(xbuf[slot], gffn_ref[...]).astype(BF16)
        g = jnp.dot(h, wg_ref[0], preferred_element_type=F32)
        u = jnp.dot(h, wu_ref[0], preferred_element_type=F32)
        a = (g * jax.nn.sigmoid(g) * u).astype(BF16)
        out_ref[...] = jnp.dot(a, wd_ref[0], preferred_element_type=F32)

    @pl.when(b >= nused)
    def _():
        out_ref[...] = jnp.zeros(out_ref.shape, F32)


def _moe(block_e, nused, slot_tok, xm_all, gffn, wg, wu, wd, *, tm, nb):
    wmap = lambda b, be, nu, tok: (be[b], 0, 0)
    grid_spec = pltpu.PrefetchScalarGridSpec(
        num_scalar_prefetch=3,
        grid=(nb,),
        in_specs=[
            pl.BlockSpec(memory_space=pl.ANY),
            pl.BlockSpec((1, D_MODEL), lambda b, be, nu, tok: (0, 0)),
            pl.BlockSpec((1, D_MODEL, D_EXPERT), wmap),
            pl.BlockSpec((1, D_MODEL, D_EXPERT), wmap),
            pl.BlockSpec((1, D_EXPERT, D_MODEL), wmap),
        ],
        out_specs=pl.BlockSpec((tm, D_MODEL), lambda b, be, nu, tok: (b, 0)),
        scratch_shapes=[pltpu.VMEM((2, tm, D_MODEL), F32), pltpu.SemaphoreType.DMA((2,))],
    )
    return pl.pallas_call(
        _moe_body,
        grid_spec=grid_spec,
        out_shape=jax.ShapeDtypeStruct((nb * tm, D_MODEL), F32),
        compiler_params=_cparams(("arbitrary",), row_dma=True),
        name="moe",
    )(block_e, nused, slot_tok, xm_all, gffn, wg, wu, wd)


def _final_body(dest_ref, xm_ref, route_ref, p_ref, mo_hbm, gple_ref, wpg_ref, bpg_ref, wpe_ref,
                y_ref, ybuf, sem, *, tok0):
    i = pl.program_id(0)
    nt = pl.num_programs(0)
    tm = y_ref.shape[0]

    def issue(tile, slot):
        def body(r, carry):
            base = 2 * (tok0 + tile * tm + r)
            for kk in range(2):
                d = dest_ref[base + kk]
                pltpu.make_async_copy(mo_hbm.at[pl.ds(d, 1)], ybuf.at[slot, kk, pl.ds(r, 1)], sem.at[slot]).start()
            return carry
        lax.fori_loop(0, tm, body, 0, unroll=DMA_UNROLL)

    @pl.when(i == 0)
    def _():
        issue(0, 0)

    slot = i % 2

    @pl.when(i + 1 < nt)
    def _():
        issue(i + 1, 1 - slot)

    for kk in range(2):
        pltpu.make_async_copy(mo_hbm.at[pl.ds(0, tm)], ybuf.at[slot, kk], sem.at[slot]).wait()
    route = route_ref[...]
    x2 = xm_ref[...] + route[:, 2:3] * ybuf[slot, 0] + route[:, 3:4] * ybuf[slot, 1]
    r = _rms(x2, gple_ref[...]).astype(BF16)
    z = jnp.dot(r, wpg_ref[...], preferred_element_type=F32) + bpg_ref[...]
    pe = jnp.dot(p_ref[...].astype(BF16), wpe_ref[...], preferred_element_type=F32)
    y_ref[...] = x2 + jax.nn.sigmoid(z) * pe


def _final(dest, xm_all, route, p2d, moe_out, gple, wpg, bpg, wpe, *, tm, blk0):
    T = p2d.shape[0]
    nt = T // tm
    const = lambda i, d: (0, 0)
    grid_spec = pltpu.PrefetchScalarGridSpec(
        num_scalar_prefetch=1,
        grid=(nt,),
        in_specs=[
            pl.BlockSpec((tm, D_MODEL), lambda i, d: (i + blk0, 0)),
            pl.BlockSpec((tm, ROUTE_W), lambda i, d: (i + blk0, 0)),
            pl.BlockSpec((tm, PLE_DIM), lambda i, d: (i, 0)),
            pl.BlockSpec(memory_space=pl.ANY),
            pl.BlockSpec((1, D_MODEL), const),
            pl.BlockSpec((D_MODEL, D_MODEL), const),
            pl.BlockSpec((1, D_MODEL), const),
            pl.BlockSpec((PLE_DIM, D_MODEL), const),
        ],
        out_specs=pl.BlockSpec((tm, D_MODEL), lambda i, d: (i, 0)),
        scratch_shapes=[pltpu.VMEM((2, 2, tm, D_MODEL), F32), pltpu.SemaphoreType.DMA((2,))],
    )
    return pl.pallas_call(
        functools.partial(_final_body, tok0=blk0 * tm),
        grid_spec=grid_spec,
        out_shape=jax.ShapeDtypeStruct((T, D_MODEL), F32),
        compiler_params=_cparams(("arbitrary",), row_dma=True),
        name="final",
    )(dest, xm_all, route, p2d, moe_out, gple, wpg, bpg, wpe)


def _pair_stack(bias):
    H, R, K = bias.shape
    return bias.reshape(H // 2, 2 * R, K)


def _toeplitz_bias(rel_bias, R, c):
    u = np.arange(c - (BAND - 1), c + R)[::-1]
    strip = rel_bias[:, np.clip(u, -REL_CLIP, REL_CLIP) + REL_CLIP]
    return jnp.stack([strip[:, R - 1 - i:R - 1 - i + BAND] for i in range(R)], axis=1)


def _prompt_bias(rel_bias):
    tbl = _toeplitz_bias(rel_bias, CHUNK, BAND - CHUNK)
    tbl = jnp.where((np.arange(BAND) >= CHUNK)[None, None, :], tbl, NEG_INF)
    return _pair_stack(tbl.astype(F32))


def _sample_bias(rel_bias, S, L_all):
    q_pos = PAST_LEN + np.arange(S)
    k_pos = PAST_LEN + S - L_all + np.arange(BAND)
    dc = q_pos[:, None] // CHUNK - k_pos[None, :] // CHUNK
    mask = (dc >= 0) & (dc <= N_PREV_CHUNKS) & (np.arange(BAND) < L_all)[None, :]
    tbl = jnp.where(mask[None], _toeplitz_bias(rel_bias, S, L_all - S), NEG_INF)
    return _pair_stack(tbl.astype(F32))


def _routing_tables(route_all, *, tm, nb):
    T_all = route_all.shape[0]
    A = 2 * T_all
    e_flat = route_all[:, 0:2].astype(jnp.int32).reshape(A)
    onehot = e_flat[:, None] == jnp.arange(N_EXPERTS, dtype=jnp.int32)[None, :]
    ch = LANES
    assert A % ch == 0 and A < (1 << 24)
    oh3 = onehot.astype(F32).reshape(A // ch, ch, N_EXPERTS)
    tri = jnp.asarray(np.tril(np.ones((ch, ch), np.float32)))
    within = jnp.einsum('ij,cjk->cik', tri, oh3)
    tot = within[:, -1, :]
    base = jnp.cumsum(tot, axis=0) - tot
    counts = (base[-1] + tot[-1]).astype(jnp.int32)
    pcounts = (counts + tm - 1) // tm * tm
    pends = jnp.cumsum(pcounts)
    pstarts = pends - pcounts
    slot = within + base[:, None, :] - 1.0 + pstarts.astype(F32)[None, None, :]
    dest = jnp.sum(jnp.where(oh3 > 0, slot, 0.0), axis=-1).astype(jnp.int32).reshape(A)
    nused = (pends[-1] // tm).astype(jnp.int32)
    blk = jnp.minimum(jnp.arange(nb, dtype=jnp.int32), nused - 1)
    block_e = jnp.minimum(jnp.sum((pends[None, :] <= (blk * tm)[:, None]).astype(jnp.int32), axis=1),
                          N_EXPERTS - 1).astype(jnp.int32)
    slot_tok = jnp.zeros((nb * tm,), jnp.int32).at[dest].set(
        jnp.arange(A, dtype=jnp.int32) // 2, unique_indices=True)
    return block_e, nused.reshape(1), slot_tok, dest


def kernel(x_prompt, x_sample, p_prompt, p_sample, cache_k, cache_v, cache_conv, g_mix, w_in, qn_g, kn_g, rel_bias, conv_w, conv_b, cln_g, cln_b, w_out, g_ffn, w_grp, b_grp, w_rt, b_rt, w_e_gate, w_e_up, w_e_down, g_ple, w_pg, b_pg, w_pe):
    depth = g_mix.shape[0]
    assert depth == 1, "single-layer step"
    B, S, D = x_prompt.shape
    Bs, Ss, _ = x_sample.shape
    Tp, Ts = B * S, Bs * Ss
    T_all = Tp + Ts
    keep_p = min(WINDOW, S)
    L_all = cache_k.shape[2] + Ss
    keep_s = min(WINDOW, L_all)
    assert S >= CONV_K - 1 and Ss >= CONV_K - 1 and L_all <= BAND

    TM = 256
    TQ = WINDOW - CHUNK
    TM_E = 512
    assert S % TQ == 0 and S % TM == 0 and Ts % TM == 0 and Tp % TM == 0

    i = 0
    w_in_b = w_in[i].astype(BF16)
    w_oa = w_out[i, :ATTN_WIDTH].astype(BF16)
    w_oc = w_out[i, ATTN_WIDTH:].astype(BF16)
    w_pg_b = w_pg[i].astype(BF16)
    w_pe_b = w_pe[i].astype(BF16)
    wg_b = w_e_gate[i].astype(BF16)
    wu_b = w_e_up[i].astype(BF16)
    wd_b = w_e_down[i].astype(BF16)
    row = lambda a: a.reshape(1, -1).astype(F32)
    gmix, gffn, gple, bpg = row(g_mix[i]), row(g_ffn[i]), row(g_ple[i]), row(b_pg[i])
    qg = row(jnp.tile(qn_g[i], N_HEADS))
    kg = row(jnp.tile(kn_g[i], N_HEADS))
    sgn = jnp.asarray(np.where(np.arange(LANES) < HEAD_DIM, 1.0, -1.0).reshape(1, LANES), F32)
    conv_w32 = jnp.pad(conv_w[i], ((0, HALO - CONV_K), (0, 0)))
    convb, clng, clnb = row(conv_b[i]), row(cln_g[i]), row(cln_b[i])
    w_r = jnp.pad(jnp.concatenate([w_grp[i], w_rt[i]], axis=1), ((0, 0), (0, ROUTE_W - N_GROUPS - N_EXPERTS)))
    wr_hi = w_r.astype(BF16)
    wr_lo = (w_r - wr_hi.astype(F32)).astype(BF16)
    b_r = jnp.pad(jnp.concatenate([b_grp[i], b_rt[i]]), (0, ROUTE_W - N_GROUPS - N_EXPERTS)).reshape(1, ROUTE_W)
    bias_p = _prompt_bias(rel_bias[i])
    bias_s = _sample_bias(rel_bias[i], Ss, L_all)

    nj = S // TM
    ntail = -(-keep_p // TM)
    tail_map = lambda t: ((t // nj) * ntail + jnp.maximum(t % nj - (nj - ntail), 0), 0)
    q_p, k_p, v_p, glu_p, kf_p, vf_p, gf_p = _in_proj(
        x_prompt.reshape(Tp, D), gmix, w_in_b, qg, kg, sgn, tm=TM,
        tail_rows=B * ntail * TM, kf_map=tail_map,
        gf_rows=HALO, gf_blocks=B, gf_map=lambda t: (t // nj, 0))
    cut = ntail * TM - keep_p
    new_k_p = kf_p.reshape(B, ntail * TM, N_HEADS, HEAD_DIM)[:, cut:][None]
    new_v_p = vf_p.reshape(B, ntail * TM, N_HEADS, HEAD_DIM)[:, cut:][None]
    new_c_p = gf_p.reshape(B, HALO, CONV_CH)[:, HALO - (CONV_K - 1):][None]

    conv_p = _conv_module(glu_p.reshape(B, S, CONV_CH), jnp.zeros((B, HALO, CONV_CH), F32),
                          conv_w32, convb, clng, clnb, ts=TM)
    attn_p = _attn_prompt(q_p, k_p, v_p, bias_p, B=B, S=S, tq=TQ)

    ident = lambda t: (t, 0)
    q_s, k_s, v_s, glu_s, kf_s, vf_s, gf_s = _in_proj(
        x_sample.reshape(Ts, D), gmix, w_in_b, qg, kg, sgn, tm=TM,
        tail_rows=Ts, kf_map=ident, gf_rows=TM, gf_blocks=Ts // TM, gf_map=ident)
    ck = cache_k[i].reshape(Bs, -1, ATTN_WIDTH)
    cv = cache_v[i].reshape(Bs, -1, ATTN_WIDTH)
    k_all = jnp.concatenate([ck, kf_s.reshape(Bs, Ss, ATTN_WIDTH)], axis=1)
    v_all = jnp.concatenate([cv, vf_s.reshape(Bs, Ss, ATTN_WIDTH)], axis=1)
    new_k_s = k_all[:, L_all - keep_s:].reshape(Bs, keep_s, N_HEADS, HEAD_DIM)[None]
    new_v_s = v_all[:, L_all - keep_s:].reshape(Bs, keep_s, N_HEADS, HEAD_DIM)[None]
    conv_in_s = jnp.concatenate([cache_conv[i], gf_s.reshape(Bs, Ss, CONV_CH)], axis=1)
    new_c_s = conv_in_s[:, -(CONV_K - 1):][None]
    padk = ((0, 0), (0, BAND - L_all), (0, 0))
    attn_s = _attn_sample(q_s.reshape(Bs, Ss, ATTN_WIDTH), jnp.pad(k_all.astype(BF16), padk),
                          jnp.pad(v_all.astype(BF16), padk), bias_s)
    halo_s = jnp.pad(cache_conv[i], ((0, 0), (HALO - (CONV_K - 1), 0), (0, 0)))
    conv_s = _conv_module(glu_s.reshape(Bs, Ss, CONV_CH), halo_s, conv_w32, convb, clng, clnb, ts=Ss)

    xm_all, route_all = _out_proj(attn_p, attn_s.reshape(Ts, ATTN_WIDTH),
                                  conv_p.reshape(Tp, CONV_CH), conv_s.reshape(Ts, CONV_CH),
                                  x_prompt.reshape(Tp, D), x_sample.reshape(Ts, D),
                                  w_oa, w_oc, gffn, wr_hi, wr_lo, b_r, tm=TM)

    nb = -(-2 * T_all // TM_E) + N_EXPERTS
    block_e, nused, slot_tok, dest = _routing_tables(route_all, tm=TM_E, nb=nb)
    moe_out = _moe(block_e, nused, slot_tok, xm_all, gffn, wg_b, wu_b, wd_b, tm=TM_E, nb=nb)

    y_p = _final(dest, xm_all, route_all, p_prompt[i].reshape(Tp, PLE_DIM), moe_out,
                 gple, w_pg_b, bpg, w_pe_b, tm=TM, blk0=0)
    y_s = _final(dest, xm_all, route_all, p_sample[i].reshape(Ts, PLE_DIM), moe_out,
                 gple, w_pg_b, bpg, w_pe_b, tm=TM, blk0=Tp // TM)
    return (y_p.reshape(B, S, D), y_s.reshape(Bs, Ss, D), new_k_p, new_v_p, new_c_p,
            new_k_s, new_v_s, new_c_s)
```
